```python
import math
import jax, jax.numpy as jnp
from jax import lax
import numpy as np

D_MODEL = 1024
BATCH = 4
SEQ = 8192
DEPTH = 1

N_MEM = 256
D_RNN = D_MODEL
N_LRU_HEADS = 8
LRU_BLOCK = D_RNN // N_LRU_HEADS
LRU_CONV = 4
LRU_C = 8.0
D_CONV = D_MODEL
SC_CONV = 3
N_XHEADS = 4
XHEAD_DIM = D_MODEL // N_XHEADS
D_FF = 3 * D_MODEL
FFN_CONV = 3
LN_EPS = 1e-5
DN_ALPHA = (2.0 * DEPTH) ** 0.25
DN_BETA = (8.0 * DEPTH) ** -0.25
IN_SPLITS = [D_RNN, D_RNN, D_CONV, D_CONV, D_CONV, D_MODEL, D_MODEL]
IN_COLS = sum(IN_SPLITS)

kernel_name = "hawk_shortconv_gated_hybrid_deepnorm"


def layer_norm(x, g, b):
    xf = x.astype(jnp.float32)
    mu = jnp.mean(xf, axis=-1, keepdims=True)
    var = jnp.mean(jnp.square(xf - mu), axis=-1, keepdims=True)
    y = (xf - mu) * lax.rsqrt(var + LN_EPS)
    return (y * g.astype(jnp.float32) + b.astype(jnp.float32)).astype(x.dtype)


def causal_dwconv(u, w):
    k_width = w.shape[0]
    s = u.shape[1]
    up = jnp.pad(u, ((0, 0), (k_width - 1, 0), (0, 0)))
    return sum(up[:, k_width - 1 - k: k_width - 1 - k + s] * w[k] for k in range(k_width))


def block_diag_linear(u, w, b):
    bsz, s, d = u.shape
    uh = u.reshape(bsz, s, w.shape[0], w.shape[1])
    return jnp.einsum('bshi,hij->bshj', uh, w).reshape(bsz, s, d) + b


def rg_lru(u, w_r, b_r, w_i, b_i, lam):
    s = u.shape[1]
    r = jax.nn.sigmoid(block_diag_linear(u, w_r, b_r).astype(jnp.float32))
    i = jax.nn.sigmoid(block_diag_linear(u, w_i, b_i).astype(jnp.float32))
    log_a = -LRU_C * r * jax.nn.softplus(-lam.astype(jnp.float32))
    a = jnp.exp(log_a)
    mult = jnp.sqrt(-jnp.expm1(2.0 * log_a))
    is_first = (jnp.arange(s) == 0)[None, :, None]
    mult = jnp.where(is_first, 1.0, mult)
    bterm = mult * (i * u.astype(jnp.float32))

    def combine(left, right):
        a_l, b_l = left
        a_r, b_r_ = right
        return a_l * a_r, a_r * b_l + b_r_

    _, h = lax.associative_scan(combine, (a, bterm), axis=1)
    return h.astype(u.dtype)


def setup_inputs(seed: int = 0) -> dict:
    key = jax.random.key(seed)
    ks = iter(jax.random.split(key, 40))

    def nrm(shape, scale):
        return jax.random.normal(next(ks), shape, jnp.float32) * scale

    def gain(shape):
        return 1.0 + nrm(shape, 0.02)

    L = DEPTH
    x = nrm((BATCH, SEQ, D_MODEL), 1.0)
    mem = nrm((BATCH, N_MEM, D_MODEL), 1.0)
    a_c = jax.random.uniform(next(ks), (L, D_RNN), jnp.float32, 0.9, 0.999)
    p = a_c ** (1.0 / LRU_C)
    lru_lambda = jnp.log(p) - jnp.log1p(-p)
    return {
        "x": x,
        "mem": mem,
        "w_in": nrm((L, D_MODEL, IN_COLS), D_MODEL ** -0.5),
        "lru_conv_w": nrm((L, LRU_CONV, D_RNN), LRU_CONV ** -0.5),
        "lru_conv_b": nrm((L, D_RNN), 0.02),
        "w_rgate": nrm((L, N_LRU_HEADS, LRU_BLOCK, LRU_BLOCK), LRU_BLOCK ** -0.5),
        "b_rgate": nrm((L, D_RNN), 0.02),
        "w_igate": nrm((L, N_LRU_HEADS, LRU_BLOCK, LRU_BLOCK), LRU_BLOCK ** -0.5),
        "b_igate": nrm((L, D_RNN), 0.02),
        "lru_lambda": lru_lambda,
        "w_lru_out": nrm((L, D_RNN, D_MODEL), DN_BETA * D_RNN ** -0.5),
        "sc_conv_w": nrm((L, SC_CONV, D_CONV), SC_CONV ** -0.5),
        "w_sc_out": nrm((L, D_CONV, D_MODEL), DN_BETA * D_CONV ** -0.5),
        "b_merge": nrm((L, 2, D_MODEL), 0.02),
        "w_mix_out": nrm((L, D_MODEL, D_MODEL), DN_BETA * D_MODEL ** -0.5),
        "ln1_g": gain((L, D_MODEL)),
        "ln1_b": nrm((L, D_MODEL), 0.02),
        "w_q": nrm((L, D_MODEL, D_MODEL), D_MODEL ** -0.5),
        "w_k": nrm((L, D_MODEL, D_MODEL), D_MODEL ** -0.5),
        "w_v": nrm((L, D_MODEL, D_MODEL), DN_BETA * D_MODEL ** -0.5),
        "w_xo": nrm((L, D_MODEL, D_MODEL), DN_BETA * D_MODEL ** -0.5),
        "ln2_g": gain((L, D_MODEL)),
        "ln2_b": nrm((L, D_MODEL), 0.02),
        "w_ffn_gate": nrm((L, D_MODEL, D_FF), D_MODEL ** -0.5),
        "w_ffn_up": nrm((L, D_MODEL, D_FF), DN_BETA * D_MODEL ** -0.5),
        "ffn_conv_w": nrm((L, FFN_CONV, D_FF), FFN_CONV ** -0.5),
        "ffn_conv_b": nrm((L, D_FF), 0.02),
        "w_ffn_down": nrm((L, D_FF, D_MODEL), DN_BETA * D_FF ** -0.5),
        "ln3_g": gain((L, D_MODEL)),
        "ln3_b": nrm((L, D_MODEL), 0.02),
    }


def reference(x, mem, w_in, lru_conv_w, lru_conv_b, w_rgate, b_rgate, w_igate, b_igate,
              lru_lambda, w_lru_out, sc_conv_w, w_sc_out, b_merge, w_mix_out, ln1_g, ln1_b,
              w_q, w_k, w_v, w_xo, ln2_g, ln2_b, w_ffn_gate, w_ffn_up, ffn_conv_w, ffn_conv_b,
              w_ffn_down, ln3_g, ln3_b):
    bsz, s, _ = x.shape
    n_mem = mem.shape[1]
    split_idx = list(np.cumsum(IN_SPLITS)[:-1])
    for l in range(DEPTH):
        proj = x @ w_in[l]
        lru_gate, lru_in, sc_b, sc_c, sc_h, g_lru, g_sc = jnp.split(proj, split_idx, axis=-1)
        u = causal_dwconv(lru_in, lru_conv_w[l]) + lru_conv_b[l]
        h = rg_lru(u, w_rgate[l], b_rgate[l], w_igate[l], b_igate[l], lru_lambda[l])
        y_lru = (jax.nn.gelu(lru_gate) * h) @ w_lru_out[l]
        y_sc = (sc_b * causal_dwconv(sc_c * sc_h, sc_conv_w[l])) @ w_sc_out[l]
        merged = (jax.nn.sigmoid(g_lru + b_merge[l, 0]) * y_lru
                  + jax.nn.sigmoid(g_sc + b_merge[l, 1]) * y_sc)
        x = layer_norm(DN_ALPHA * x + merged @ w_mix_out[l], ln1_g[l], ln1_b[l])

        q = (x @ w_q[l]).reshape(bsz, s, N_XHEADS, XHEAD_DIM)
        k = (mem @ w_k[l]).reshape(bsz, n_mem, N_XHEADS, XHEAD_DIM)
        v = (mem @ w_v[l]).reshape(bsz, n_mem, N_XHEADS, XHEAD_DIM)
        scores = jnp.einsum('bshd,bmhd->bhsm', q, k).astype(jnp.float32) * (XHEAD_DIM ** -0.5)
        probs = jax.nn.softmax(scores, axis=-1).astype(v.dtype)
        att = jnp.einsum('bhsm,bmhd->bshd', probs, v).reshape(bsz, s, D_MODEL)
        x = layer_norm(DN_ALPHA * x + att @ w_xo[l], ln2_g[l], ln2_b[l])

        gate = causal_dwconv(x @ w_ffn_gate[l], ffn_conv_w[l]) + ffn_conv_b[l]
        ffn = (jax.nn.gelu(gate) * (x @ w_ffn_up[l])) @ w_ffn_down[l]
        x = layer_norm(DN_ALPHA * x + ffn, ln3_g[l], ln3_b[l])
    return x
```

```python
import functools
import math

import jax
import jax.numpy as jnp
from jax import lax
from jax.experimental import pallas as pl
from jax.experimental.pallas import tpu as pltpu

D_MODEL = 1024
N_LRU_HEADS = 8
LRU_BLOCK = D_MODEL // N_LRU_HEADS
LRU_CONV = 4
LRU_C = 8.0
SC_CONV = 3
N_XHEADS = 4
XHEAD_DIM = D_MODEL // N_XHEADS
D_FF = 3 * D_MODEL
FFN_CONV = 3
LN_EPS = 1e-5
DEPTH = 1
DN_ALPHA = (2.0 * DEPTH) ** 0.25

SUBLANES = 8
HALO = SUBLANES
SEQ_TILE = 256
VMEM_LIMIT_BYTES = 56 * 1024 * 1024

F32 = jnp.float32
BF16 = jnp.bfloat16


def _dot(a, b):
    return jnp.dot(a, b, preferred_element_type=F32)


def _layer_norm(y, g, b):
    mu = jnp.mean(y, axis=-1, keepdims=True)
    yc = y - mu
    var = jnp.mean(yc * yc, axis=-1, keepdims=True)
    return yc * lax.rsqrt(var + LN_EPS) * g + b


def _causal_conv(buf_ref, cur, w_ref, width, rows):
    buf_ref[HALO:HALO + rows, :] = cur
    y = cur * w_ref[0:1, :]
    for k in range(1, width):
        y = y + buf_ref[HALO - k:HALO - k + rows, :] * w_ref[k:k + 1, :]
    buf_ref[0:HALO, :] = buf_ref[rows:rows + HALO, :]
    return y


def _mixer_kernel(x_ref, w_in_ref, cw_ref, cb_ref, w_ri_ref, b_r_ref, b_i_ref, lam_ref,
                  w_lru_out_ref, scw_ref, w_sc_out_ref, b_merge_ref, w_mix_ref, g_ref, b_ref,
                  o_ref, lin_buf, sc_buf, a_buf, b_buf, h_buf, h_carry):
    rows = x_ref.shape[0]
    groups = rows // SUBLANES
    t = pl.program_id(1)

    @pl.when(t == 0)
    def _():
        lin_buf[0:HALO, :] = jnp.zeros((HALO, D_MODEL), F32)
        sc_buf[0:HALO, :] = jnp.zeros((HALO, D_MODEL), F32)
        h_carry[...] = jnp.zeros_like(h_carry)

    x = x_ref[...]
    xb = x.astype(BF16)

    def seg(i):
        return _dot(xb, w_in_ref[:, i * D_MODEL:(i + 1) * D_MODEL])

    u = _causal_conv(lin_buf, seg(1), cw_ref, LRU_CONV, rows) + cb_ref[...]
    ub = u.astype(BF16)
    z = [_dot(ub[:, h * LRU_BLOCK:(h + 1) * LRU_BLOCK], w_ri_ref[h]) for h in range(N_LRU_HEADS)]
    zr = jnp.concatenate([zh[:, :LRU_BLOCK] for zh in z], axis=-1) + b_r_ref[...]
    zi = jnp.concatenate([zh[:, LRU_BLOCK:] for zh in z], axis=-1) + b_i_ref[...]
    r = jax.nn.sigmoid(zr)
    gi = jax.nn.sigmoid(zi)
    nlam = -lam_ref[...]
    softplus = jnp.maximum(nlam, 0.0) + jnp.log1p(jnp.exp(-jnp.abs(nlam)))
    log_a = (-LRU_C * softplus) * r
    a = jnp.exp(log_a)
    th = jnp.tanh(log_a)
    mult = jnp.sqrt(-2.0 * th / (1.0 - th))
    row = lax.broadcasted_iota(jnp.int32, (rows, D_MODEL), 0)
    mult = jnp.where((row == 0) & (t == 0), 1.0, mult)
    bterm = mult * (gi * u)

    a3 = a.reshape(groups, SUBLANES, D_MODEL)
    b3 = bterm.reshape(groups, SUBLANES, D_MODEL)
    sub = lax.broadcasted_iota(jnp.int32, (groups, SUBLANES, D_MODEL), 1)
    for k in (1, 2, 4):
        keep = sub >= k
        b3 = jnp.where(keep, b3 + a3 * pltpu.roll(b3, k, axis=1), b3)
        a3 = jnp.where(keep, a3 * pltpu.roll(a3, k, axis=1), a3)
    a_buf[...] = a3
    b_buf[...] = b3

    def carry(g, h_prev):
        hg = b_buf[g] + a_buf[g] * h_prev
        h_buf[g] = hg
        return hg[SUBLANES - 1:SUBLANES, :]

    h_carry[...] = lax.fori_loop(0, groups, carry, h_carry[...], unroll=4)
    h = h_buf[...].reshape(rows, D_MODEL)

    y_lru = _dot((jax.nn.gelu(seg(0), approximate=True) * h).astype(BF16), w_lru_out_ref[...])

    ch = seg(3) * seg(4)
    y_sc = _dot((seg(2) * _causal_conv(sc_buf, ch, scw_ref, SC_CONV, rows)).astype(BF16),
                w_sc_out_ref[...])

    merged = (jax.nn.sigmoid(seg(5) + b_merge_ref[0:1, :]) * y_lru
              + jax.nn.sigmoid(seg(6) + b_merge_ref[1:2, :]) * y_sc)
    y = DN_ALPHA * x + _dot(merged.astype(BF16), w_mix_ref[...])
    o_ref[...] = _layer_norm(y, g_ref[...], b_ref[...])


def _xattn_kernel(x_ref, mem_ref, w_q_ref, w_k_ref, w_v_ref, w_xo_ref, g_ref, b_ref,
                  o_ref, k_scr, v_scr):
    t = pl.program_id(1)

    @pl.when(t == 0)
    def _():
        mb = mem_ref[...].astype(BF16)
        k_scr[...] = _dot(mb, w_k_ref[...]).astype(BF16)
        v_scr[...] = _dot(mb, w_v_ref[...]).astype(BF16)

    x = x_ref[...]
    q = (_dot(x.astype(BF16), w_q_ref[...]) * (XHEAD_DIM ** -0.5)).astype(BF16)
    heads = []
    for h in range(N_XHEADS):
        sl = slice(h * XHEAD_DIM, (h + 1) * XHEAD_DIM)
        s = lax.dot_general(q[:, sl], k_scr[:, sl], (((1,), (1,)), ((), ())),
                            preferred_element_type=F32)
        p = jnp.exp(s - jnp.max(s, axis=-1, keepdims=True))
        denom = jnp.sum(p, axis=-1, keepdims=True)
        heads.append(_dot(p.astype(BF16), v_scr[:, sl]) / denom)
    att = jnp.concatenate(heads, axis=-1)
    y = DN_ALPHA * x + _dot(att.astype(BF16), w_xo_ref[...])
    o_ref[...] = _layer_norm(y, g_ref[...], b_ref[...])


def _ffn_kernel(x_ref, w_gate_ref, w_up_ref, cw_ref, cb_ref, w_down_ref, g_ref, b_ref,
                o_ref, gate_buf):
    rows = x_ref.shape[0]
    t = pl.program_id(1)

    @pl.when(t == 0)
    def _():
        gate_buf[0:HALO, :] = jnp.zeros((HALO, D_FF), F32)

    x = x_ref[...]
    xb = x.astype(BF16)
    acc = DN_ALPHA * x
    for j in range(D_FF // D_MODEL):
        sl = slice(j * D_MODEL, (j + 1) * D_MODEL)
        gbuf = gate_buf.at[:, sl]
        gate = _causal_conv(gbuf, _dot(xb, w_gate_ref[:, sl]), cw_ref.at[:, sl], FFN_CONV, rows)
        gate = gate + cb_ref[:, sl]
        hidden = jax.nn.gelu(gate, approximate=True) * _dot(xb, w_up_ref[:, sl])
        acc = acc + _dot(hidden.astype(BF16), w_down_ref[sl, :])
    o_ref[...] = _layer_norm(acc, g_ref[...], b_ref[...])


def _resident(shape):
    zeros = (0,) * len(shape)
    return pl.BlockSpec(shape, lambda b, t: zeros, pipeline_mode=pl.Buffered(1))


def _seq_tiled(rows, cols):
    return pl.BlockSpec((None, rows, cols), lambda b, t: (b, t, 0))


def _call(body, name, x, params, scratch_shapes, extra_specs=()):
    bsz, seq, d = x.shape
    assert seq % SEQ_TILE == 0 and d == D_MODEL
    in_specs = [_seq_tiled(SEQ_TILE, d), *extra_specs, *[_resident(p.shape) for p in params]]
    return pl.pallas_call(
        body,
        name=name,
        grid=(bsz, seq // SEQ_TILE),
        in_specs=in_specs,
        out_specs=_seq_tiled(SEQ_TILE, d),
        out_shape=jax.ShapeDtypeStruct(x.shape, x.dtype),
        scratch_shapes=scratch_shapes,
        compiler_params=pltpu.CompilerParams(
            dimension_semantics=("arbitrary", "arbitrary"),
            vmem_limit_bytes=VMEM_LIMIT_BYTES),
    )


def _row(v):
    return v.reshape(1, -1)


def kernel(x, mem, w_in, lru_conv_w, lru_conv_b, w_rgate, b_rgate, w_igate, b_igate, lru_lambda,
           w_lru_out, sc_conv_w, w_sc_out, b_merge, w_mix_out, ln1_g, ln1_b, w_q, w_k, w_v, w_xo,
           ln2_g, ln2_b, w_ffn_gate, w_ffn_up, ffn_conv_w, ffn_conv_b, w_ffn_down, ln3_g, ln3_b):
    n_mem = mem.shape[1]
    groups = SEQ_TILE // SUBLANES
    for l in range(DEPTH):
        w_ri = jnp.concatenate([w_rgate[l], w_igate[l]], axis=-1).astype(BF16)
        mixer_params = (
            w_in[l].astype(BF16), lru_conv_w[l], _row(lru_conv_b[l]), w_ri, _row(b_rgate[l]),
            _row(b_igate[l]), _row(lru_lambda[l]), w_lru_out[l].astype(BF16), sc_conv_w[l],
            w_sc_out[l].astype(BF16), b_merge[l], w_mix_out[l].astype(BF16), _row(ln1_g[l]),
            _row(ln1_b[l]))
        x = _call(_mixer_kernel, "mixer", x, mixer_params, [
            pltpu.VMEM((SEQ_TILE + HALO, D_MODEL), F32),
            pltpu.VMEM((SEQ_TILE + HALO, D_MODEL), F32),
            pltpu.VMEM((groups, SUBLANES, D_MODEL), F32),
            pltpu.VMEM((groups, SUBLANES, D_MODEL), F32),
            pltpu.VMEM((groups, SUBLANES, D_MODEL), F32),
            pltpu.VMEM((1, D_MODEL), F32),
        ])(x, *mixer_params)

        xattn_params = (w_q[l].astype(BF16), w_k[l].astype(BF16), w_v[l].astype(BF16),
                        w_xo[l].astype(BF16), _row(ln2_g[l]), _row(ln2_b[l]))
        mem_spec = pl.BlockSpec((None, n_mem, D_MODEL), lambda b, t: (b, 0, 0))
        x = _call(_xattn_kernel, "xattn", x, xattn_params, [
            pltpu.VMEM((n_mem, D_MODEL), BF16),
            pltpu.VMEM((n_mem, D_MODEL), BF16),
        ], extra_specs=(mem_spec,))(x, mem, *xattn_params)

        ffn_params = (w_ffn_gate[l].astype(BF16), w_ffn_up[l].astype(BF16), ffn_conv_w[l],
                      _row(ffn_conv_b[l]), w_ffn_down[l].astype(BF16), _row(ln3_g[l]),
                      _row(ln3_b[l]))
        x = _call(_ffn_kernel, "convffn", x, ffn_params, [
            pltpu.VMEM((SEQ_TILE + HALO, D_FF), F32),
        ])(x, *ffn_params)
    return x
```

```python
import jax
import jax.numpy as jnp
from jax import lax
from jax.experimental import pallas as pl
from jax.experimental.pallas import tpu as pltpu

D_MODEL = 1024
N_LRU_HEADS = 8
LRU_BLOCK = D_MODEL // N_LRU_HEADS
LRU_CONV = 4
LRU_C = 8.0
SC_CONV = 3
N_XHEADS = 4
XHEAD_DIM = D_MODEL // N_XHEADS
D_FF = 3 * D_MODEL
FFN_CONV = 3
LN_EPS = 1e-5
DEPTH = 1
DN_ALPHA = (2.0 * DEPTH) ** 0.25

SUBLANES = 8
HALO = SUBLANES
SEQ_TILE = 256
BATCH_TILE = 2
LRU_CHUNKS = 4
VMEM_LIMIT_BYTES = 60 * 1024 * 1024

F32 = jnp.float32
BF16 = jnp.bfloat16


def _dot(a, b):
    return jnp.dot(a, b, preferred_element_type=F32)


def _layer_norm(y, g, b):
    mu = jnp.mean(y, axis=-1, keepdims=True)
    yc = y - mu
    var = jnp.mean(yc * yc, axis=-1, keepdims=True)
    return yc * lax.rsqrt(var + LN_EPS) * g + b


def _causal_conv(buf_ref, cur, w_ref, width, rows):
    buf_ref[HALO:HALO + rows, :] = cur
    y = cur * w_ref[0:1, :]
    for k in range(1, width):
        y = y + buf_ref[HALO - k:HALO - k + rows, :] * w_ref[k:k + 1, :]
    buf_ref[0:HALO, :] = buf_ref[rows:rows + HALO, :]
    return y


def _run_interleaved(chains):
    chains = list(chains)
    while chains:
        for chain in list(chains):
            if next(chain, StopIteration) is StopIteration:
                chains.remove(chain)


def _mixer_kernel(x_ref, w_in_ref, cw_ref, cb_ref, w_ri_ref, b_r_ref, b_i_ref, lam_ref,
                  w_lru_out_ref, scw_ref, w_sc_out_ref, b_merge_ref, w_mix_ref, g_ref, b_ref,
                  o_ref, lin_buf, sc_buf, h_buf, h_carry):
    rows = x_ref.shape[1]
    crows = rows // LRU_CHUNKS
    cgroups = crows // SUBLANES
    t = pl.program_id(1)

    @pl.when(t == 0)
    def _():
        lin_buf[:, 0:HALO, :] = jnp.zeros((BATCH_TILE, HALO, D_MODEL), F32)
        sc_buf[:, 0:HALO, :] = jnp.zeros((BATCH_TILE, HALO, D_MODEL), F32)
        h_carry[...] = jnp.zeros_like(h_carry)

    nlam = -lam_ref[...]
    softplus = jnp.maximum(nlam, 0.0) + jnp.log1p(jnp.exp(-jnp.abs(nlam)))
    log_a_scale = -LRU_C * softplus

    def chain(s):
        x = x_ref[s]
        xb = x.astype(BF16)

        def seg(i):
            return _dot(xb, w_in_ref[:, i * D_MODEL:(i + 1) * D_MODEL])

        lin = seg(1)
        yield
        u = _causal_conv(lin_buf.at[s], lin, cw_ref, LRU_CONV, rows) + cb_ref[...]
        ub = u.astype(BF16)
        yield
        z = [_dot(ub[:, h * LRU_BLOCK:(h + 1) * LRU_BLOCK], w_ri_ref[h])
             for h in range(N_LRU_HEADS)]
        zr = jnp.concatenate([zh[:, :LRU_BLOCK] for zh in z], axis=-1) + b_r_ref[...]
        zi = jnp.concatenate([zh[:, LRU_BLOCK:] for zh in z], axis=-1) + b_i_ref[...]
        sc_c = seg(3)
        yield

        def lru_chunk(c, h_prev):
            rs = slice(c * crows, (c + 1) * crows)
            r = jax.nn.sigmoid(zr[rs])
            gi = jax.nn.sigmoid(zi[rs])
            log_a = log_a_scale * r
            a = jnp.exp(log_a)
            th = jnp.tanh(log_a)
            mult = jnp.sqrt(-2.0 * th / (1.0 - th))
            if c == 0:
                row = lax.broadcasted_iota(jnp.int32, (crows, D_MODEL), 0)
                mult = jnp.where((row == 0) & (t == 0), 1.0, mult)
            bterm = mult * (gi * u[rs])
            a3 = a.reshape(cgroups, SUBLANES, D_MODEL)
            b3 = bterm.reshape(cgroups, SUBLANES, D_MODEL)
            sub = lax.broadcasted_iota(jnp.int32, (cgroups, SUBLANES, D_MODEL), 1)
            for k in (1, 2, 4):
                keep = sub >= k
                b3 = jnp.where(keep, b3 + a3 * pltpu.roll(b3, k, axis=1), b3)
                a3 = jnp.where(keep, a3 * pltpu.roll(a3, k, axis=1), a3)
            for g in range(cgroups):
                hg = b3[g] + a3[g] * h_prev
                h_buf[s, c * cgroups + g] = hg
                h_prev = hg[SUBLANES - 1:SUBLANES, :]
            return h_prev

        h_prev = lru_chunk(0, h_carry[s])
        yield
        sc_h = seg(4)
        yield
        h_prev = lru_chunk(1, h_prev)
        yield
        sc_b = seg(2)
        yield
        h_prev = lru_chunk(2, h_prev)
        yield
        g_sc = seg(6)
        yield
        h_carry[s] = lru_chunk(3, h_prev)
        yield
        g_lru = seg(5)
        yield
        conv_ch = _causal_conv(sc_buf.at[s], sc_c * sc_h, scw_ref, SC_CONV, rows)
        sc_in = (sc_b * conv_ch).astype(BF16)
        yield
        y_sc = _dot(sc_in, w_sc_out_ref[...])
        lru_gate = seg(0)
        yield
        h = h_buf[s].reshape(rows, D_MODEL)
        lru_in = (jax.nn.gelu(lru_gate, approximate=True) * h).astype(BF16)
        yield
        y_lru = _dot(lru_in, w_lru_out_ref[...])
        yield
        merged = (jax.nn.sigmoid(g_lru + b_merge_ref[0:1, :]) * y_lru
                  + jax.nn.sigmoid(g_sc + b_merge_ref[1:2, :]) * y_sc).astype(BF16)
        yield
        y = DN_ALPHA * x + _dot(merged, w_mix_ref[...])
        yield
        o_ref[s] = _layer_norm(y, g_ref[...], b_ref[...])

    _run_interleaved(chain(s) for s in range(BATCH_TILE))


def _xattn_kernel(x_ref, mem_ref, w_q_ref, w_k_ref, w_v_ref, w_xo_ref, g_ref, b_ref,
                  o_ref, k_scr, v_scr):
    t = pl.program_id(1)

    @pl.when(t == 0)
    def _():
        for s in range(BATCH_TILE):
            mb = mem_ref[s].astype(BF16)
            k_scr[s] = _dot(mb, w_k_ref[...]).astype(BF16)
            v_scr[s] = _dot(mb, w_v_ref[...]).astype(BF16)

    def chain(s):
        x = x_ref[s]
        q = _dot(x.astype(BF16), w_q_ref[...])
        yield
        qb = (q * (XHEAD_DIM ** -0.5)).astype(BF16)
        yield
        head_cols = [slice(h * XHEAD_DIM, (h + 1) * XHEAD_DIM) for h in range(N_XHEADS)]
        scores = [lax.dot_general(qb[:, sl], k_scr[s, :, sl], (((1,), (1,)), ((), ())),
                                  preferred_element_type=F32) for sl in head_cols]
        yield
        probs, denoms = [], []
        for sc in scores:
            p = jnp.exp(sc - jnp.max(sc, axis=-1, keepdims=True))
            denoms.append(jnp.sum(p, axis=-1, keepdims=True))
            probs.append(p.astype(BF16))
        yield
        pv = [_dot(p, v_scr[s, :, sl]) for p, sl in zip(probs, head_cols)]
        yield
        att = jnp.concatenate([o / d for o, d in zip(pv, denoms)], axis=-1).astype(BF16)
        yield
        y = DN_ALPHA * x + _dot(att, w_xo_ref[...])
        yield
        o_ref[s] = _layer_norm(y, g_ref[...], b_ref[...])

    _run_interleaved(chain(s) for s in range(BATCH_TILE))


def _ffn_kernel(x_ref, w_gate_ref, w_up_ref, cw_ref, cb_ref, w_down_ref, g_ref, b_ref,
                o_ref, gate_buf):
    rows = x_ref.shape[1]
    t = pl.program_id(1)
    n_col = D_FF // D_MODEL
    cols = [slice(j * D_MODEL, (j + 1) * D_MODEL) for j in range(n_col)]

    @pl.when(t == 0)
    def _():
        gate_buf[:, 0:HALO, :] = jnp.zeros((BATCH_TILE, HALO, D_FF), F32)

    def chain(s):
        x = x_ref[s]
        xb = x.astype(BF16)

        def gate_act(j, pre):
            sl = cols[j]
            gate = _causal_conv(gate_buf.at[s, :, sl], pre, cw_ref.at[:, sl], FFN_CONV, rows)
            return jax.nn.gelu(gate + cb_ref[:, sl], approximate=True)

        acc = DN_ALPHA * x
        pre = _dot(xb, w_gate_ref[:, cols[0]])
        yield
        for j in range(n_col):
            act = gate_act(j, pre)
            yield
            up = _dot(xb, w_up_ref[:, cols[j]])
            if j + 1 < n_col:
                pre = _dot(xb, w_gate_ref[:, cols[j + 1]])
            yield
            hidden = (act * up).astype(BF16)
            yield
            acc = acc + _dot(hidden, w_down_ref[cols[j], :])
        yield
        o_ref[s] = _layer_norm(acc, g_ref[...], b_ref[...])

    _run_interleaved(chain(s) for s in range(BATCH_TILE))


def _resident(shape):
    zeros = (0,) * len(shape)
    return pl.BlockSpec(shape, lambda b, t: zeros, pipeline_mode=pl.Buffered(1))


def _seq_tiled(rows, cols):
    return pl.BlockSpec((BATCH_TILE, rows, cols), lambda b, t: (b, t, 0))


def _call(body, name, x, params, scratch_shapes, extra_specs=()):
    bsz, seq, d = x.shape
    assert bsz % BATCH_TILE == 0 and seq % SEQ_TILE == 0 and d == D_MODEL
    in_specs = [_seq_tiled(SEQ_TILE, d), *extra_specs, *[_resident(p.shape) for p in params]]
    return pl.pallas_call(
        body,
        name=name,
        grid=(bsz // BATCH_TILE, seq // SEQ_TILE),
        in_specs=in_specs,
        out_specs=_seq_tiled(SEQ_TILE, d),
        out_shape=jax.ShapeDtypeStruct(x.shape, x.dtype),
        scratch_shapes=scratch_shapes,
        compiler_params=pltpu.CompilerParams(
            dimension_semantics=("arbitrary", "arbitrary"),
            vmem_limit_bytes=VMEM_LIMIT_BYTES),
    )


def _row(v):
    return v.reshape(1, -1)


def kernel(x, mem, w_in, lru_conv_w, lru_conv_b, w_rgate, b_rgate, w_igate, b_igate, lru_lambda,
           w_lru_out, sc_conv_w, w_sc_out, b_merge, w_mix_out, ln1_g, ln1_b, w_q, w_k, w_v, w_xo,
           ln2_g, ln2_b, w_ffn_gate, w_ffn_up, ffn_conv_w, ffn_conv_b, w_ffn_down, ln3_g, ln3_b):
    n_mem = mem.shape[1]
    groups = SEQ_TILE // SUBLANES
    for l in range(DEPTH):
        w_ri = jnp.concatenate([w_rgate[l], w_igate[l]], axis=-1).astype(BF16)
        mixer_params = (
            w_in[l].astype(BF16), lru_conv_w[l], _row(lru_conv_b[l]), w_ri, _row(b_rgate[l]),
            _row(b_igate[l]), _row(lru_lambda[l]), w_lru_out[l].astype(BF16), sc_conv_w[l],
            w_sc_out[l].astype(BF16), b_merge[l], w_mix_out[l].astype(BF16), _row(ln1_g[l]),
            _row(ln1_b[l]))
        x = _call(_mixer_kernel, "mixer", x, mixer_params, [
            pltpu.VMEM((BATCH_TILE, SEQ_TILE + HALO, D_MODEL), F32),
            pltpu.VMEM((BATCH_TILE, SEQ_TILE + HALO, D_MODEL), F32),
            pltpu.VMEM((BATCH_TILE, groups, SUBLANES, D_MODEL), F32),
            pltpu.VMEM((BATCH_TILE, 1, D_MODEL), F32),
        ])(x, *mixer_params)

        xattn_params = (w_q[l].astype(BF16), w_k[l].astype(BF16), w_v[l].astype(BF16),
                        w_xo[l].astype(BF16), _row(ln2_g[l]), _row(ln2_b[l]))
        mem_spec = pl.BlockSpec((BATCH_TILE, n_mem, D_MODEL), lambda b, t: (b, 0, 0))
        x = _call(_xattn_kernel, "xattn", x, xattn_params, [
            pltpu.VMEM((BATCH_TILE, n_mem, D_MODEL), BF16),
            pltpu.VMEM((BATCH_TILE, n_mem, D_MODEL), BF16),
        ], extra_specs=(mem_spec,))(x, mem, *xattn_params)

        ffn_params = (w_ffn_gate[l].astype(BF16), w_ffn_up[l].astype(BF16), ffn_conv_w[l],
                      _row(ffn_conv_b[l]), w_ffn_down[l].astype(BF16), _row(ln3_g[l]),
                      _row(ln3_b[l]))
        x = _call(_ffn_kernel, "convffn", x, ffn_params, [
            pltpu.VMEM((BATCH_TILE, SEQ_TILE + HALO, D_FF), F32),
        ])(x, *ffn_params)
    return x
```

```python
import jax
import jax.numpy as jnp
from jax import lax
from jax.experimental import pallas as pl
from jax.experimental.pallas import tpu as pltpu

D_MODEL = 1024
N_LRU_HEADS = 8
LRU_BLOCK = D_MODEL // N_LRU_HEADS
LRU_CONV = 4
LRU_C = 8.0
SC_CONV = 3
N_XHEADS = 4
XHEAD_DIM = D_MODEL // N_XHEADS
D_FF = 3 * D_MODEL
FFN_CONV = 3
LN_EPS = 1e-5
DEPTH = 1
DN_ALPHA = (2.0 * DEPTH) ** 0.25

SUBLANES = 8
HALO = SUBLANES
SEQ_TILE = 256
BATCH_TILE = 2
LRU_CHUNKS = 4
VMEM_LIMIT_BYTES = 60 * 1024 * 1024

F32 = jnp.float32
BF16 = jnp.bfloat16


def _dot(a, b):
    return jnp.dot(a, b, preferred_element_type=F32)


def _sigmoid(z):
    return 0.5 * jnp.tanh(0.5 * z) + 0.5


def _after(x, produced):
    bits = pltpu.bitcast(produced[0:1, :x.shape[-1]], jnp.uint32)
    return x + pltpu.bitcast((bits >> 16) >> 16, F32)


def _layer_norm(y, g, b):
    mu = jnp.mean(y, axis=-1, keepdims=True)
    yc = y - mu
    var = jnp.mean(yc * yc, axis=-1, keepdims=True)
    return yc * lax.rsqrt(var + LN_EPS) * g + b


def _causal_conv(buf_ref, cur, w_ref, width, rows):
    buf_ref[HALO:HALO + rows, :] = cur
    y = cur * w_ref[0:1, :]
    for k in range(1, width):
        y = y + buf_ref[HALO - k:HALO - k + rows, :] * w_ref[k:k + 1, :]
    buf_ref[0:HALO, :] = buf_ref[rows:rows + HALO, :]
    return y


def _run_interleaved(chains):
    chains = list(chains)
    while chains:
        for chain in list(chains):
            if next(chain, StopIteration) is StopIteration:
                chains.remove(chain)


def _row_permutation(rows):
    p = jnp.arange(rows)
    src = (p % SUBLANES) * (rows // SUBLANES) + p // SUBLANES
    perm = (src[:, None] == p[None, :]).astype(BF16)
    return perm, perm.T


def _interleaved_conv(tail_ref, cur, w_ref, width):
    rows, d = cur.shape
    taps = width - 1
    tail = cur[rows - taps * SUBLANES:]
    sub = lax.broadcasted_iota(jnp.int32, (taps, SUBLANES, d), 1)
    before = jnp.where(sub == 0,
                       pltpu.roll(tail_ref[...].reshape(taps, SUBLANES, d), 1, axis=1),
                       pltpu.roll(tail.reshape(taps, SUBLANES, d), 1, axis=1))
    before = before.reshape(taps * SUBLANES, d)
    tail_ref[...] = tail
    y = cur * w_ref[0:1, :]
    for k in range(1, width):
        shifted = jnp.concatenate(
            [before[(taps - k) * SUBLANES:], cur[:rows - k * SUBLANES]], axis=0)
        y = y + shifted * w_ref[k:k + 1, :]
    return y


def _mixer_kernel(x_ref, perm_ref, unperm_ref, w_in_ref, cw_ref, cb_ref, w_ri_ref, b_r_ref,
                  b_i_ref, lam_ref, w_lru_out_ref, scw_ref, w_sc_out_ref, b_merge_ref, w_mix_ref,
                  g_ref, b_ref, o_ref, lin_tail, sc_tail, hl_buf, ac_buf, h_carry):
    rows = x_ref.shape[1]
    crows = rows // LRU_CHUNKS
    cgroups = crows // SUBLANES
    t = pl.program_id(1)

    @pl.when(t == 0)
    def _():
        lin_tail[...] = jnp.zeros_like(lin_tail)
        sc_tail[...] = jnp.zeros_like(sc_tail)
        h_carry[...] = jnp.zeros_like(h_carry)

    nlam = -lam_ref[...]
    softplus = jnp.maximum(nlam, 0.0) + jnp.log1p(jnp.exp(-jnp.abs(nlam)))
    log_a_scale = -LRU_C * softplus

    xb_all = jnp.concatenate(
        [_dot(perm_ref[...], x_ref[s].astype(BF16)).astype(BF16) for s in range(BATCH_TILE)],
        axis=0)
    seg_cache = {}

    def chain(s):
        def seg(i):
            if i not in seg_cache:
                seg_cache[i] = _dot(xb_all, w_in_ref[:, i * D_MODEL:(i + 1) * D_MODEL])
            return seg_cache[i][s * rows:(s + 1) * rows]

        lin = seg(1)
        yield
        u = _interleaved_conv(lin_tail.at[s], lin, cw_ref, LRU_CONV) + cb_ref[...]
        ub = u.astype(BF16)
        yield
        z = [_dot(ub[:, h * LRU_BLOCK:(h + 1) * LRU_BLOCK], w_ri_ref[h])
             for h in range(N_LRU_HEADS)]
        zr = jnp.concatenate([zh[:, :LRU_BLOCK] for zh in z], axis=-1) + b_r_ref[...]
        zi = jnp.concatenate([zh[:, LRU_BLOCK:] for zh in z], axis=-1) + b_i_ref[...]
        sc_c = seg(3)
        yield

        def lru_chunk(c, state, produced=None):
            rs = slice(c * crows, (c + 1) * crows)
            scale = log_a_scale if produced is None else _after(log_a_scale, produced)
            log_a = scale * _sigmoid(zr[rs])
            a = jnp.exp(log_a)
            th = jnp.tanh(log_a)
            mult = jnp.sqrt(-2.0 * th / (1.0 - th))
            if c == 0:
                row = lax.broadcasted_iota(jnp.int32, (crows, D_MODEL), 0)
                mult = jnp.where((row == 0) & (t == 0), 1.0, mult)
            bterm = mult * (_sigmoid(zi[rs]) * u[rs])
            a3 = a.reshape(cgroups, SUBLANES, D_MODEL)
            b3 = bterm.reshape(cgroups, SUBLANES, D_MODEL)
            for g in range(cgroups):
                if state is None:
                    hl, ac = b3[g], a3[g]
                else:
                    hl, ac = a3[g] * state[0] + b3[g], a3[g] * state[1]
                hl_buf[s, c * cgroups + g] = hl
                ac_buf[s, c * cgroups + g] = ac
                state = (hl, ac)
            return state

        state = lru_chunk(0, None, sc_c)
        yield
        sc_h = seg(4)
        sc_b = seg(2)
        yield
        state = lru_chunk(1, state, sc_b)
        conv_ch = _interleaved_conv(sc_tail.at[s], sc_c * sc_h, scw_ref, SC_CONV)
        sc_in = (sc_b * conv_ch).astype(BF16)
        yield
        y_sc = _dot(sc_in, w_sc_out_ref[...])
        g_sc = seg(6)
        yield
        state = lru_chunk(2, state, g_sc)
        yield
        lru_gate = seg(0)
        yield
        q, p = lru_chunk(3, state, lru_gate)
        sub = lax.broadcasted_iota(jnp.int32, (SUBLANES, D_MODEL), 0)
        for k in (1, 2, 4):
            keep = sub >= k
            q = jnp.where(keep, q + p * pltpu.roll(q, k, axis=0), q)
            p = jnp.where(keep, p * pltpu.roll(p, k, axis=0), p)
        h0 = h_carry[s]
        after = p * h0 + q
        entering = jnp.where(sub == 0, h0, pltpu.roll(after, 1, axis=0))
        h_carry[s] = after[SUBLANES - 1:SUBLANES, :]
        h = (hl_buf[s] + ac_buf[s] * entering).reshape(rows, D_MODEL)
        lru_in = (jax.nn.gelu(lru_gate, approximate=True) * h).astype(BF16)
        yield
        g_lru = seg(5)
        y_lru = _dot(lru_in, w_lru_out_ref[...])
        yield
        merged = (_sigmoid(g_lru + b_merge_ref[0:1, :]) * y_lru
                  + _sigmoid(g_sc + b_merge_ref[1:2, :]) * y_sc).astype(BF16)
        yield
        merged = _dot(unperm_ref[...], merged).astype(BF16)
        y = DN_ALPHA * x_ref[s] + _dot(merged, w_mix_ref[...])
        yield
        o_ref[s] = _layer_norm(y, g_ref[...], b_ref[...])

    _run_interleaved(chain(s) for s in range(BATCH_TILE))


def _xattn_kernel(x_ref, mem_ref, w_q_ref, w_k_ref, w_v_ref, w_xo_ref, g_ref, b_ref,
                  o_ref, k_scr, v_scr):
    t = pl.program_id(1)

    @pl.when(t == 0)
    def _():
        for s in range(BATCH_TILE):
            mb = mem_ref[s].astype(BF16)
            k_scr[s] = _dot(mb, w_k_ref[...]).astype(BF16)
            v_scr[s] = _dot(mb, w_v_ref[...]).astype(BF16)

    def chain(s):
        x = x_ref[s]
        q = _dot(x.astype(BF16), w_q_ref[...])
        yield
        qb = (q * (XHEAD_DIM ** -0.5)).astype(BF16)
        yield
        head_cols = [slice(h * XHEAD_DIM, (h + 1) * XHEAD_DIM) for h in range(N_XHEADS)]
        scores = [lax.dot_general(qb[:, sl], k_scr[s, :, sl], (((1,), (1,)), ((), ())),
                                  preferred_element_type=F32) for sl in head_cols]
        yield
        probs, denoms = [], []
        for sc in scores:
            p = jnp.exp(sc - jnp.max(sc, axis=-1, keepdims=True))
            denoms.append(jnp.sum(p, axis=-1, keepdims=True))
            probs.append(p.astype(BF16))
        yield
        pv = [_dot(p, v_scr[s, :, sl]) for p, sl in zip(probs, head_cols)]
        yield
        att = jnp.concatenate([o / d for o, d in zip(pv, denoms)], axis=-1).astype(BF16)
        yield
        y = DN_ALPHA * x + _dot(att, w_xo_ref[...])
        yield
        o_ref[s] = _layer_norm(y, g_ref[...], b_ref[...])

    _run_interleaved(chain(s) for s in range(BATCH_TILE))


def _ffn_kernel(x_ref, w_gate_ref, w_up_ref, cw_ref, cb_ref, w_down_ref, g_ref, b_ref,
                o_ref, gate_buf):
    rows = x_ref.shape[1]
    t = pl.program_id(1)
    n_col = D_FF // D_MODEL
    cols = [slice(j * D_MODEL, (j + 1) * D_MODEL) for j in range(n_col)]

    @pl.when(t == 0)
    def _():
        gate_buf[:, 0:HALO, :] = jnp.zeros((BATCH_TILE, HALO, D_FF), F32)

    def chain(s):
        x = x_ref[s]
        xb = x.astype(BF16)

        def gate_act(j, pre):
            sl = cols[j]
            gate = _causal_conv(gate_buf.at[s, :, sl], pre, cw_ref.at[:, sl], FFN_CONV, rows)
            return jax.nn.gelu(gate + cb_ref[:, sl], approximate=True)

        acc = DN_ALPHA * x
        pre = _dot(xb, w_gate_ref[:, cols[0]])
        yield
        for j in range(n_col):
            act = gate_act(j, pre)
            yield
            up = _dot(xb, w_up_ref[:, cols[j]])
            if j + 1 < n_col:
                pre = _dot(xb, w_gate_ref[:, cols[j + 1]])
            yield
            hidden = (act * up).astype(BF16)
            yield
            acc = acc + _dot(hidden, w_down_ref[cols[j], :])
        yield
        o_ref[s] = _layer_norm(acc, g_ref[...], b_ref[...])

    _run_interleaved(chain(s) for s in range(BATCH_TILE))


def _resident(shape):
    zeros = (0,) * len(shape)
    return pl.BlockSpec(shape, lambda b, t: zeros, pipeline_mode=pl.Buffered(1))


def _seq_tiled(rows, cols):
    return pl.BlockSpec((BATCH_TILE, rows, cols), lambda b, t: (b, t, 0))


def _call(body, name, x, params, scratch_shapes, extra_specs=()):
    bsz, seq, d = x.shape
    assert bsz % BATCH_TILE == 0 and seq % SEQ_TILE == 0 and d == D_MODEL
    in_specs = [_seq_tiled(SEQ_TILE, d), *extra_specs, *[_resident(p.shape) for p in params]]
    return pl.pallas_call(
        body,
        name=name,
        grid=(bsz // BATCH_TILE, seq // SEQ_TILE),
        in_specs=in_specs,
        out_specs=_seq_tiled(SEQ_TILE, d),
        out_shape=jax.ShapeDtypeStruct(x.shape, x.dtype),
        scratch_shapes=scratch_shapes,
        compiler_params=pltpu.CompilerParams(
            dimension_semantics=("arbitrary", "arbitrary"),
            vmem_limit_bytes=VMEM_LIMIT_BYTES),
    )


def _row(v):
    return v.reshape(1, -1)


def kernel(x, mem, w_in, lru_conv_w, lru_conv_b, w_rgate, b_rgate, w_igate, b_igate, lru_lambda,
           w_lru_out, sc_conv_w, w_sc_out, b_merge, w_mix_out, ln1_g, ln1_b, w_q, w_k, w_v, w_xo,
           ln2_g, ln2_b, w_ffn_gate, w_ffn_up, ffn_conv_w, ffn_conv_b, w_ffn_down, ln3_g, ln3_b):
    n_mem = mem.shape[1]
    groups = SEQ_TILE // SUBLANES
    for l in range(DEPTH):
        w_ri = jnp.concatenate([w_rgate[l], w_igate[l]], axis=-1).astype(BF16)
        mixer_params = (
            *_row_permutation(SEQ_TILE), w_in[l].astype(BF16), lru_conv_w[l], _row(lru_conv_b[l]), w_ri, _row(b_rgate[l]),
            _row(b_igate[l]), _row(lru_lambda[l]), w_lru_out[l].astype(BF16), sc_conv_w[l],
            w_sc_out[l].astype(BF16), b_merge[l], w_mix_out[l].astype(BF16), _row(ln1_g[l]),
            _row(ln1_b[l]))
        x = _call(_mixer_kernel, "mixer", x, mixer_params, [
            pltpu.VMEM((BATCH_TILE, (LRU_CONV - 1) * SUBLANES, D_MODEL), F32),
            pltpu.VMEM((BATCH_TILE, (SC_CONV - 1) * SUBLANES, D_MODEL), F32),
            pltpu.VMEM((BATCH_TILE, groups, SUBLANES, D_MODEL), F32),
            pltpu.VMEM((BATCH_TILE, groups, SUBLANES, D_MODEL), F32),
            pltpu.VMEM((BATCH_TILE, 1, D_MODEL), F32),
        ])(x, *mixer_params)

        xattn_params = (w_q[l].astype(BF16), w_k[l].astype(BF16), w_v[l].astype(BF16),
                        w_xo[l].astype(BF16), _row(ln2_g[l]), _row(ln2_b[l]))
        mem_spec = pl.BlockSpec((BATCH_TILE, n_mem, D_MODEL), lambda b, t: (b, 0, 0))
        x = _call(_xattn_kernel, "xattn", x, xattn_params, [
            pltpu.VMEM((BATCH_TILE, n_mem, D_MODEL), BF16),
            pltpu.VMEM((BATCH_TILE, n_mem, D_MODEL), BF16),
        ], extra_specs=(mem_spec,))(x, mem, *xattn_params)

        ffn_params = (w_ffn_gate[l].astype(BF16), w_ffn_up[l].astype(BF16), ffn_conv_w[l],
                      _row(ffn_conv_b[l]), w_ffn_down[l].astype(BF16), _row(ln3_g[l]),
                      _row(ln3_b[l]))
        x = _call(_ffn_kernel, "convffn", x, ffn_params, [
            pltpu.VMEM((BATCH_TILE, SEQ_TILE + HALO, D_FF), F32),
        ])(x, *ffn_params)
    return x
```

```python
import jax
import jax.numpy as jnp
from jax import lax
from jax.experimental import pallas as pl
from jax.experimental.pallas import tpu as pltpu

D_MODEL = 1024
N_LRU_HEADS = 8
LRU_BLOCK = D_MODEL // N_LRU_HEADS
LRU_CONV = 4
LRU_C = 8.0
SC_CONV = 3
N_XHEADS = 4
XHEAD_DIM = D_MODEL // N_XHEADS
D_FF = 3 * D_MODEL
FFN_CONV = 3
LN_EPS = 1e-5
DEPTH = 1
DN_ALPHA = (2.0 * DEPTH) ** 0.25

SUBLANES = 8
SEQ_TILE = 256
BATCH_TILE = 2
LRU_CHUNKS = 4
LANES = 128
LANE_BLOCKS = D_MODEL // LANES
RUN = SEQ_TILE // SUBLANES
RUN_PITCH = RUN + SUBLANES
VMEM_LIMIT_BYTES = 60 * 1024 * 1024

F32 = jnp.float32
BF16 = jnp.bfloat16


def _dot(a, b):
    return jnp.dot(a, b, preferred_element_type=F32)


def _sigmoid(z):
    return 0.5 * jnp.tanh(0.5 * z) + 0.5


def _after(x, produced):
    bits = pltpu.bitcast(produced[0:1, :x.shape[-1]], jnp.uint32)
    return x + pltpu.bitcast((bits >> 16) >> 16, F32)


def _layer_norm(y, g, b):
    mu = jnp.mean(y, axis=-1, keepdims=True)
    yc = y - mu
    var = jnp.mean(yc * yc, axis=-1, keepdims=True)
    return yc * lax.rsqrt(var + LN_EPS) * g + b


def _run_interleaved(chains):
    chains = list(chains)
    while chains:
        for chain in list(chains):
            if next(chain, StopIteration) is StopIteration:
                chains.remove(chain)


def _to_interleaved(stage_ref, x):
    for c in range(LANE_BLOCKS):
        for i in range(SUBLANES):
            stage_ref[c, i * RUN_PITCH:i * RUN_PITCH + RUN, :] = (
                x[i * RUN:(i + 1) * RUN, c * LANES:(c + 1) * LANES])
    return jnp.concatenate(
        [jnp.concatenate([stage_ref[c, pl.ds(j, SUBLANES, stride=RUN_PITCH), :]
                          for c in range(LANE_BLOCKS)], axis=-1) for j in range(RUN)], axis=0)


def _from_interleaved(stage_ref, y):
    for j in range(RUN):
        for c in range(LANE_BLOCKS):
            stage_ref[c, pl.ds(j, SUBLANES, stride=RUN_PITCH), :] = (
                y[j * SUBLANES:(j + 1) * SUBLANES, c * LANES:(c + 1) * LANES])
    return jnp.concatenate(
        [jnp.concatenate([stage_ref[c, i * RUN_PITCH:i * RUN_PITCH + RUN, :]
                          for c in range(LANE_BLOCKS)], axis=-1) for i in range(SUBLANES)], axis=0)


def _interleaved_conv(tail_ref, cur, w_ref, width):
    rows, d = cur.shape
    taps = width - 1
    tail = cur[rows - taps * SUBLANES:]
    sub = lax.broadcasted_iota(jnp.int32, (taps, SUBLANES, d), 1)
    before = jnp.where(sub == 0,
                       pltpu.roll(tail_ref[...].reshape(taps, SUBLANES, d), 1, axis=1),
                       pltpu.roll(tail.reshape(taps, SUBLANES, d), 1, axis=1))
    before = before.reshape(taps * SUBLANES, d)
    tail_ref[...] = tail
    y = cur * w_ref[0:1, :]
    for k in range(1, width):
        shifted = jnp.concatenate(
            [before[(taps - k) * SUBLANES:], cur[:rows - k * SUBLANES]], axis=0)
        y = y + shifted * w_ref[k:k + 1, :]
    return y


def _mixer_kernel(x_ref, w_in_ref, cw_ref, cb_ref, w_ri_ref, b_r_ref, b_i_ref, lam_ref,
                  w_lru_out_ref, scw_ref, w_sc_out_ref, b_merge_ref, w_mix_ref, g_ref, b_ref,
                  o_ref, stage, lin_tail, sc_tail, hl_buf, ac_buf, h_carry):
    rows = x_ref.shape[1]
    crows = rows // LRU_CHUNKS
    cgroups = crows // SUBLANES
    t = pl.program_id(1)

    @pl.when(t == 0)
    def _():
        lin_tail[...] = jnp.zeros_like(lin_tail)
        sc_tail[...] = jnp.zeros_like(sc_tail)
        h_carry[...] = jnp.zeros_like(h_carry)

    nlam = -lam_ref[...]
    softplus = jnp.maximum(nlam, 0.0) + jnp.log1p(jnp.exp(-jnp.abs(nlam)))
    log_a_scale = -LRU_C * softplus

    def chain(s):
        x = _to_interleaved(stage.at[s], x_ref[s])
        xb = x.astype(BF16)

        def seg(i):
            return _dot(xb, w_in_ref[:, i * D_MODEL:(i + 1) * D_MODEL])

        lin = seg(1)
        yield
        u = _interleaved_conv(lin_tail.at[s], lin, cw_ref, LRU_CONV) + cb_ref[...]
        ub = u.astype(BF16)
        yield
        z = [_dot(ub[:, h * LRU_BLOCK:(h + 1) * LRU_BLOCK], w_ri_ref[h])
             for h in range(N_LRU_HEADS)]
        zr = jnp.concatenate([zh[:, :LRU_BLOCK] for zh in z], axis=-1) + b_r_ref[...]
        zi = jnp.concatenate([zh[:, LRU_BLOCK:] for zh in z], axis=-1) + b_i_ref[...]
        sc_c = seg(3)
        yield

        def lru_chunk(c, state, produced):
            rs = slice(c * crows, (c + 1) * crows)
            log_a = _after(log_a_scale, produced) * _sigmoid(zr[rs])
            a = jnp.exp(log_a)
            th = jnp.tanh(log_a)
            mult = jnp.sqrt(-2.0 * th / (1.0 - th))
            if c == 0:
                row = lax.broadcasted_iota(jnp.int32, (crows, D_MODEL), 0)
                mult = jnp.where((row == 0) & (t == 0), 1.0, mult)
            bterm = mult * (_sigmoid(zi[rs]) * u[rs])
            a3 = a.reshape(cgroups, SUBLANES, D_MODEL)
            b3 = bterm.reshape(cgroups, SUBLANES, D_MODEL)
            for g in range(cgroups):
                if state is None:
                    hl, ac = b3[g], a3[g]
                else:
                    hl, ac = a3[g] * state[0] + b3[g], a3[g] * state[1]
                hl_buf[s, c * cgroups + g] = hl
                ac_buf[s, c * cgroups + g] = ac
                state = (hl, ac)
            return state

        state = lru_chunk(0, None, sc_c)
        yield
        sc_h = seg(4)
        sc_b = seg(2)
        yield
        state = lru_chunk(1, state, sc_b)
        conv_ch = _interleaved_conv(sc_tail.at[s], sc_c * sc_h, scw_ref, SC_CONV)
        sc_in = (sc_b * conv_ch).astype(BF16)
        yield
        y_sc = _dot(sc_in, w_sc_out_ref[...])
        g_sc = seg(6)
        yield
        state = lru_chunk(2, state, g_sc)
        yield
        lru_gate = seg(0)
        yield
        q, p = lru_chunk(3, state, lru_gate)
        sub = lax.broadcasted_iota(jnp.int32, (SUBLANES, D_MODEL), 0)
        for k in (1, 2, 4):
            keep = sub >= k
            q = jnp.where(keep, q + p * pltpu.roll(q, k, axis=0), q)
            p = jnp.where(keep, p * pltpu.roll(p, k, axis=0), p)
        h0 = h_carry[s]
        after = p * h0 + q
        entering = jnp.where(sub == 0, h0, pltpu.roll(after, 1, axis=0))
        h_carry[s] = after[SUBLANES - 1:SUBLANES, :]
        h = (hl_buf[s] + ac_buf[s] * entering).reshape(rows, D_MODEL)
        lru_in = (jax.nn.gelu(lru_gate, approximate=True) * h).astype(BF16)
        yield
        g_lru = seg(5)
        y_lru = _dot(lru_in, w_lru_out_ref[...])
        yield
        merged = (_sigmoid(g_lru + b_merge_ref[0:1, :]) * y_lru
                  + _sigmoid(g_sc + b_merge_ref[1:2, :]) * y_sc).astype(BF16)
        yield
        y = DN_ALPHA * x + _dot(merged, w_mix_ref[...])
        yield
        o_ref[s] = _layer_norm(y, g_ref[...], b_ref[...])

    _run_interleaved(chain(s) for s in range(BATCH_TILE))


def _xattn_kernel(x_ref, mem_ref, w_q_ref, w_k_ref, w_v_ref, w_xo_ref, g_ref, b_ref,
                  o_ref, k_scr, v_scr):
    t = pl.program_id(1)

    @pl.when(t == 0)
    def _():
        for s in range(BATCH_TILE):
            mb = mem_ref[s].astype(BF16)
            k_scr[s] = _dot(mb, w_k_ref[...]).astype(BF16)
            v_scr[s] = _dot(mb, w_v_ref[...]).astype(BF16)

    def chain(s):
        x = x_ref[s]
        q = _dot(x.astype(BF16), w_q_ref[...])
        yield
        qb = (q * (XHEAD_DIM ** -0.5)).astype(BF16)
        yield
        head_cols = [slice(h * XHEAD_DIM, (h + 1) * XHEAD_DIM) for h in range(N_XHEADS)]
        scores = [lax.dot_general(qb[:, sl], k_scr[s, :, sl], (((1,), (1,)), ((), ())),
                                  preferred_element_type=F32) for sl in head_cols]
        yield
        probs, denoms = [], []
        for sc in scores:
            p = jnp.exp(sc - jnp.max(sc, axis=-1, keepdims=True))
            denoms.append(jnp.sum(p, axis=-1, keepdims=True))
            probs.append(p.astype(BF16))
        yield
        pv = [_dot(p, v_scr[s, :, sl]) for p, sl in zip(probs, head_cols)]
        yield
        att = jnp.concatenate([o / d for o, d in zip(pv, denoms)], axis=-1).astype(BF16)
        yield
        y = DN_ALPHA * x + _dot(att, w_xo_ref[...])
        yield
        o_ref[s] = _layer_norm(y, g_ref[...], b_ref[...])

    _run_interleaved(chain(s) for s in range(BATCH_TILE))


def _ffn_kernel(x_ref, w_gate_ref, w_up_ref, cw_ref, cb_ref, w_down_ref, g_ref, b_ref,
                o_ref, stage, gate_tail):
    t = pl.program_id(1)
    n_col = D_FF // D_MODEL
    cols = [slice(j * D_MODEL, (j + 1) * D_MODEL) for j in range(n_col)]

    @pl.when(t == 0)
    def _():
        gate_tail[...] = jnp.zeros_like(gate_tail)

    def chain(s):
        x = x_ref[s]
        xb = x.astype(BF16)

        def gate_act(j, pre):
            sl = cols[j]
            gate = _interleaved_conv(gate_tail.at[s, :, sl], pre, cw_ref.at[:, sl], FFN_CONV)
            return jax.nn.gelu(gate + cb_ref[:, sl], approximate=True)

        acc = DN_ALPHA * x
        pre = _dot(xb, w_gate_ref[:, cols[0]])
        yield
        for j in range(n_col):
            act = gate_act(j, pre)
            yield
            up = _dot(xb, w_up_ref[:, cols[j]])
            if j + 1 < n_col:
                pre = _dot(xb, w_gate_ref[:, cols[j + 1]])
            yield
            hidden = (act * up).astype(BF16)
            yield
            acc = acc + _dot(hidden, w_down_ref[cols[j], :])
        yield
        o_ref[s] = _from_interleaved(stage.at[s], _layer_norm(acc, g_ref[...], b_ref[...]))

    _run_interleaved(chain(s) for s in range(BATCH_TILE))


def _resident(shape):
    zeros = (0,) * len(shape)
    return pl.BlockSpec(shape, lambda b, t: zeros, pipeline_mode=pl.Buffered(1))


def _seq_tiled(rows, cols):
    return pl.BlockSpec((BATCH_TILE, rows, cols), lambda b, t: (b, t, 0))


def _call(body, name, x, params, scratch_shapes, extra_specs=()):
    bsz, seq, d = x.shape
    assert bsz % BATCH_TILE == 0 and seq % SEQ_TILE == 0 and d == D_MODEL
    in_specs = [_seq_tiled(SEQ_TILE, d), *extra_specs, *[_resident(p.shape) for p in params]]
    return pl.pallas_call(
        body,
        name=name,
        grid=(bsz // BATCH_TILE, seq // SEQ_TILE),
        in_specs=in_specs,
        out_specs=_seq_tiled(SEQ_TILE, d),
        out_shape=jax.ShapeDtypeStruct(x.shape, x.dtype),
        scratch_shapes=scratch_shapes,
        compiler_params=pltpu.CompilerParams(
            dimension_semantics=("arbitrary", "arbitrary"),
            vmem_limit_bytes=VMEM_LIMIT_BYTES),
    )


def _row(v):
    return v.reshape(1, -1)


def kernel(x, mem, w_in, lru_conv_w, lru_conv_b, w_rgate, b_rgate, w_igate, b_igate, lru_lambda,
           w_lru_out, sc_conv_w, w_sc_out, b_merge, w_mix_out, ln1_g, ln1_b, w_q, w_k, w_v, w_xo,
           ln2_g, ln2_b, w_ffn_gate, w_ffn_up, ffn_conv_w, ffn_conv_b, w_ffn_down, ln3_g, ln3_b):
    n_mem = mem.shape[1]
    groups = SEQ_TILE // SUBLANES
    for l in range(DEPTH):
        w_ri = jnp.concatenate([w_rgate[l], w_igate[l]], axis=-1).astype(BF16)
        mixer_params = (
            w_in[l].astype(BF16), lru_conv_w[l], _row(lru_conv_b[l]), w_ri, _row(b_rgate[l]),
            _row(b_igate[l]), _row(lru_lambda[l]), w_lru_out[l].astype(BF16), sc_conv_w[l],
            w_sc_out[l].astype(BF16), b_merge[l], w_mix_out[l].astype(BF16), _row(ln1_g[l]),
            _row(ln1_b[l]))
        stage = pltpu.VMEM((BATCH_TILE, LANE_BLOCKS, SUBLANES * RUN_PITCH, LANES), F32)
        x = _call(_mixer_kernel, "mixer", x, mixer_params, [
            stage,
            pltpu.VMEM((BATCH_TILE, (LRU_CONV - 1) * SUBLANES, D_MODEL), F32),
            pltpu.VMEM((BATCH_TILE, (SC_CONV - 1) * SUBLANES, D_MODEL), F32),
            pltpu.VMEM((BATCH_TILE, groups, SUBLANES, D_MODEL), F32),
            pltpu.VMEM((BATCH_TILE, groups, SUBLANES, D_MODEL), F32),
            pltpu.VMEM((BATCH_TILE, 1, D_MODEL), F32),
        ])(x, *mixer_params)

        xattn_params = (w_q[l].astype(BF16), w_k[l].astype(BF16), w_v[l].astype(BF16),
                        w_xo[l].astype(BF16), _row(ln2_g[l]), _row(ln2_b[l]))
        mem_spec = pl.BlockSpec((BATCH_TILE, n_mem, D_MODEL), lambda b, t: (b, 0, 0))
        x = _call(_xattn_kernel, "xattn", x, xattn_params, [
            pltpu.VMEM((BATCH_TILE, n_mem, D_MODEL), BF16),
            pltpu.VMEM((BATCH_TILE, n_mem, D_MODEL), BF16),
        ], extra_specs=(mem_spec,))(x, mem, *xattn_params)

        ffn_params = (w_ffn_gate[l].astype(BF16), w_ffn_up[l].astype(BF16), ffn_conv_w[l],
                      _row(ffn_conv_b[l]), w_ffn_down[l].astype(BF16), _row(ln3_g[l]),
                      _row(ln3_b[l]))
        x = _call(_ffn_kernel, "convffn", x, ffn_params, [
            stage,
            pltpu.VMEM((BATCH_TILE, (FFN_CONV - 1) * SUBLANES, D_FF), F32),
        ])(x, *ffn_params)
    return x
```

```python
import jax
import jax.numpy as jnp
from jax import lax
from jax.experimental import pallas as pl
from jax.experimental.pallas import tpu as pltpu

D_MODEL = 1024
N_LRU_HEADS = 8
LRU_BLOCK = D_MODEL // N_LRU_HEADS
LRU_CONV = 4
LRU_C = 8.0
SC_CONV = 3
N_XHEADS = 4
XHEAD_DIM = D_MODEL // N_XHEADS
D_FF = 3 * D_MODEL
FFN_CONV = 3
LN_EPS = 1e-5
DEPTH = 1
DN_ALPHA = (2.0 * DEPTH) ** 0.25

SUBLANES = 8
SEQ_TILE = 256
BATCH_TILE = 2
LRU_CHUNKS = 4
LANES = 128
LANE_BLOCKS = D_MODEL // LANES
RUN = SEQ_TILE // SUBLANES
RUN_PITCH = RUN + SUBLANES
VMEM_LIMIT_BYTES = 60 * 1024 * 1024

F32 = jnp.float32
BF16 = jnp.bfloat16


def _dot(a, b):
    return jnp.dot(a, b, preferred_element_type=F32)


def _sigmoid(z):
    return 0.5 * jnp.tanh(0.5 * z) + 0.5


def _after(x, produced):
    bits = pltpu.bitcast(produced[0:1, :x.shape[-1]], jnp.uint32)
    return x + pltpu.bitcast((bits >> 16) >> 16, F32)


def _layer_norm(y, g, b):
    mu = jnp.mean(y, axis=-1, keepdims=True)
    yc = y - mu
    var = jnp.mean(yc * yc, axis=-1, keepdims=True)
    return yc * lax.rsqrt(var + LN_EPS) * g + b


def _run_interleaved(chains):
    chains = list(chains)
    while chains:
        for chain in list(chains):
            if next(chain, StopIteration) is StopIteration:
                chains.remove(chain)


def _to_interleaved(stage_ref, x):
    for c in range(LANE_BLOCKS):
        for i in range(SUBLANES):
            stage_ref[c, i * RUN_PITCH:i * RUN_PITCH + RUN, :] = (
                x[i * RUN:(i + 1) * RUN, c * LANES:(c + 1) * LANES])
    return jnp.concatenate(
        [jnp.concatenate([stage_ref[c, pl.ds(j, SUBLANES, stride=RUN_PITCH), :]
                          for c in range(LANE_BLOCKS)], axis=-1) for j in range(RUN)], axis=0)


def _from_interleaved(stage_ref, y):
    for j in range(RUN):
        for c in range(LANE_BLOCKS):
            stage_ref[c, pl.ds(j, SUBLANES, stride=RUN_PITCH), :] = (
                y[j * SUBLANES:(j + 1) * SUBLANES, c * LANES:(c + 1) * LANES])
    return jnp.concatenate(
        [jnp.concatenate([stage_ref[c, i * RUN_PITCH:i * RUN_PITCH + RUN, :]
                          for c in range(LANE_BLOCKS)], axis=-1) for i in range(SUBLANES)], axis=0)


def _interleaved_conv(tail_ref, cur, w_ref, width):
    rows, d = cur.shape
    taps = width - 1
    tail = cur[rows - taps * SUBLANES:]
    sub = lax.broadcasted_iota(jnp.int32, (taps, SUBLANES, d), 1)
    before = jnp.where(sub == 0,
                       pltpu.roll(tail_ref[...].reshape(taps, SUBLANES, d), 1, axis=1),
                       pltpu.roll(tail.reshape(taps, SUBLANES, d), 1, axis=1))
    before = before.reshape(taps * SUBLANES, d)
    tail_ref[...] = tail
    y = cur * w_ref[0:1, :]
    for k in range(1, width):
        shifted = jnp.concatenate(
            [before[(taps - k) * SUBLANES:], cur[:rows - k * SUBLANES]], axis=0)
        y = y + shifted * w_ref[k:k + 1, :]
    return y


def _mixer_kernel(x_ref, w_in_ref, cw_ref, cb_ref, w_ri_ref, b_r_ref, b_i_ref, lam_ref,
                  w_lru_out_ref, scw_ref, w_sc_out_ref, b_merge_ref, w_mix_ref, g_ref, b_ref,
                  o_ref, stage, lin_tail, sc_tail, hl_buf, ac_buf, h_carry):
    rows = x_ref.shape[1]
    crows = rows // LRU_CHUNKS
    cgroups = crows // SUBLANES
    t = pl.program_id(1)

    @pl.when(t == 0)
    def _():
        lin_tail[...] = jnp.zeros_like(lin_tail)
        sc_tail[...] = jnp.zeros_like(sc_tail)
        h_carry[...] = jnp.zeros_like(h_carry)

    nlam = -lam_ref[...]
    softplus = jnp.maximum(nlam, 0.0) + jnp.log1p(jnp.exp(-jnp.abs(nlam)))
    log_a_scale = -LRU_C * softplus

    def chain(s):
        x = _to_interleaved(stage.at[s], x_ref[s])
        xb = x.astype(BF16)

        def seg(i):
            return _dot(xb, w_in_ref[:, i * D_MODEL:(i + 1) * D_MODEL])

        lin = seg(1)
        yield
        u = _interleaved_conv(lin_tail.at[s], lin, cw_ref, LRU_CONV) + cb_ref[...]
        ub = u.astype(BF16)
        yield
        z = [_dot(ub[:, h * LRU_BLOCK:(h + 1) * LRU_BLOCK], w_ri_ref[h])
             for h in range(N_LRU_HEADS)]
        zr = jnp.concatenate([zh[:, :LRU_BLOCK] for zh in z], axis=-1) + b_r_ref[...]
        zi = jnp.concatenate([zh[:, LRU_BLOCK:] for zh in z], axis=-1) + b_i_ref[...]
        sc_c = seg(3)
        yield

        def lru_chunk(c, state, produced):
            rs = slice(c * crows, (c + 1) * crows)
            log_a = _after(log_a_scale, produced) * _sigmoid(zr[rs])
            a = jnp.exp(log_a)
            th = jnp.tanh(log_a)
            gain = -2.0 * th / (1.0 - th)
            mult = jnp.where(gain > 0.0, gain * lax.rsqrt(gain), 0.0)
            if c == 0:
                row = lax.broadcasted_iota(jnp.int32, (crows, D_MODEL), 0)
                mult = jnp.where((row == 0) & (t == 0), 1.0, mult)
            bterm = mult * (_sigmoid(zi[rs]) * u[rs])
            a3 = a.reshape(cgroups, SUBLANES, D_MODEL)
            b3 = bterm.reshape(cgroups, SUBLANES, D_MODEL)
            for g in range(cgroups):
                if state is None:
                    hl, ac = b3[g], a3[g]
                else:
                    hl, ac = a3[g] * state[0] + b3[g], a3[g] * state[1]
                hl_buf[s, c * cgroups + g] = hl
                ac_buf[s, c * cgroups + g] = ac
                state = (hl, ac)
            return state

        state = lru_chunk(0, None, sc_c)
        yield
        sc_h = seg(4)
        sc_b = seg(2)
        yield
        state = lru_chunk(1, state, sc_b)
        conv_ch = _interleaved_conv(sc_tail.at[s], sc_c * sc_h, scw_ref, SC_CONV)
        sc_in = (sc_b * conv_ch).astype(BF16)
        yield
        y_sc = _dot(sc_in, w_sc_out_ref[...])
        g_sc = seg(6)
        yield
        state = lru_chunk(2, state, g_sc)
        yield
        lru_gate = seg(0)
        yield
        q, p = lru_chunk(3, state, lru_gate)
        sub = lax.broadcasted_iota(jnp.int32, (SUBLANES, D_MODEL), 0)
        for k in (1, 2, 4):
            keep = sub >= k
            q = jnp.where(keep, q + p * pltpu.roll(q, k, axis=0), q)
            p = jnp.where(keep, p * pltpu.roll(p, k, axis=0), p)
        h0 = h_carry[s]
        after = p * h0 + q
        entering = jnp.where(sub == 0, h0, pltpu.roll(after, 1, axis=0))
        h_carry[s] = after[SUBLANES - 1:SUBLANES, :]
        h = (hl_buf[s] + ac_buf[s] * entering).reshape(rows, D_MODEL)
        lru_in = (jax.nn.gelu(lru_gate, approximate=True) * h).astype(BF16)
        yield
        g_lru = seg(5)
        y_lru = _dot(lru_in, w_lru_out_ref[...])
        yield
        merged = (_sigmoid(g_lru + b_merge_ref[0:1, :]) * y_lru
                  + _sigmoid(g_sc + b_merge_ref[1:2, :]) * y_sc).astype(BF16)
        yield
        y = DN_ALPHA * x + _dot(merged, w_mix_ref[...])
        yield
        o_ref[s] = _layer_norm(y, g_ref[...], b_ref[...])

    _run_interleaved(chain(s) for s in range(BATCH_TILE))


def _xattn_ffn_kernel(x_ref, mem_ref, w_q_ref, w_k_ref, w_v_ref, w_xo_ref, g2_ref, b2_ref,
                      w_gate_ref, w_up_ref, cw_ref, cb_ref, w_down_ref, g3_ref, b3_ref,
                      o_ref, k_scr, v_scr, stage, gate_tail):
    t = pl.program_id(1)
    n_col = D_FF // D_MODEL
    cols = [slice(j * D_MODEL, (j + 1) * D_MODEL) for j in range(n_col)]
    head_cols = [slice(h * XHEAD_DIM, (h + 1) * XHEAD_DIM) for h in range(N_XHEADS)]

    @pl.when(t == 0)
    def _():
        gate_tail[...] = jnp.zeros_like(gate_tail)
        for s in range(BATCH_TILE):
            mb = mem_ref[s].astype(BF16)
            k_scr[s] = _dot(mb, w_k_ref[...]).astype(BF16)
            v_scr[s] = _dot(mb, w_v_ref[...]).astype(BF16)

    def chain(s):
        x = x_ref[s]
        q = _dot(x.astype(BF16), w_q_ref[...])
        yield
        qb = (q * (XHEAD_DIM ** -0.5)).astype(BF16)
        yield
        scores = [lax.dot_general(qb[:, sl], k_scr[s, :, sl], (((1,), (1,)), ((), ())),
                                  preferred_element_type=F32) for sl in head_cols]
        yield
        probs, denoms = [], []
        for sc in scores:
            p = jnp.exp(sc - jnp.max(sc, axis=-1, keepdims=True))
            denoms.append(jnp.sum(p, axis=-1, keepdims=True))
            probs.append(p.astype(BF16))
        yield
        pv = [_dot(p, v_scr[s, :, sl]) for p, sl in zip(probs, head_cols)]
        yield
        att = jnp.concatenate([o / d for o, d in zip(pv, denoms)], axis=-1).astype(BF16)
        yield
        y = DN_ALPHA * x + _dot(att, w_xo_ref[...])
        yield
        x = _layer_norm(y, g2_ref[...], b2_ref[...])
        xb = x.astype(BF16)
        yield

        def gate_act(j, pre):
            sl = cols[j]
            gate = _interleaved_conv(gate_tail.at[s, :, sl], pre, cw_ref.at[:, sl], FFN_CONV)
            return jax.nn.gelu(gate + cb_ref[:, sl], approximate=True)

        acc = DN_ALPHA * x
        pre = _dot(xb, w_gate_ref[:, cols[0]])
        yield
        for j in range(n_col):
            act = gate_act(j, pre)
            yield
            up = _dot(xb, w_up_ref[:, cols[j]])
            if j + 1 < n_col:
                pre = _dot(xb, w_gate_ref[:, cols[j + 1]])
            yield
            hidden = (act * up).astype(BF16)
            yield
            acc = acc + _dot(hidden, w_down_ref[cols[j], :])
        yield
        o_ref[s] = _from_interleaved(stage.at[s], _layer_norm(acc, g3_ref[...], b3_ref[...]))

    _run_interleaved(chain(s) for s in range(BATCH_TILE))


def _resident(shape):
    zeros = (0,) * len(shape)
    return pl.BlockSpec(shape, lambda b, t: zeros, pipeline_mode=pl.Buffered(1))


def _seq_tiled(rows, cols):
    return pl.BlockSpec((BATCH_TILE, rows, cols), lambda b, t: (b, t, 0))


def _call(body, name, x, params, scratch_shapes, extra_specs=()):
    bsz, seq, d = x.shape
    assert bsz % BATCH_TILE == 0 and seq % SEQ_TILE == 0 and d == D_MODEL
    in_specs = [_seq_tiled(SEQ_TILE, d), *extra_specs, *[_resident(p.shape) for p in params]]
    return pl.pallas_call(
        body,
        name=name,
        grid=(bsz // BATCH_TILE, seq // SEQ_TILE),
        in_specs=in_specs,
        out_specs=_seq_tiled(SEQ_TILE, d),
        out_shape=jax.ShapeDtypeStruct(x.shape, x.dtype),
        scratch_shapes=scratch_shapes,
        compiler_params=pltpu.CompilerParams(
            dimension_semantics=("arbitrary", "arbitrary"),
            vmem_limit_bytes=VMEM_LIMIT_BYTES),
    )


def _row(v):
    return v.reshape(1, -1)


def kernel(x, mem, w_in, lru_conv_w, lru_conv_b, w_rgate, b_rgate, w_igate, b_igate, lru_lambda,
           w_lru_out, sc_conv_w, w_sc_out, b_merge, w_mix_out, ln1_g, ln1_b, w_q, w_k, w_v, w_xo,
           ln2_g, ln2_b, w_ffn_gate, w_ffn_up, ffn_conv_w, ffn_conv_b, w_ffn_down, ln3_g, ln3_b):
    n_mem = mem.shape[1]
    groups = SEQ_TILE // SUBLANES
    for l in range(DEPTH):
        w_ri = jnp.concatenate([w_rgate[l], w_igate[l]], axis=-1).astype(BF16)
        mixer_params = (
            w_in[l].astype(BF16), lru_conv_w[l], _row(lru_conv_b[l]), w_ri, _row(b_rgate[l]),
            _row(b_igate[l]), _row(lru_lambda[l]), w_lru_out[l].astype(BF16), sc_conv_w[l],
            w_sc_out[l].astype(BF16), b_merge[l], w_mix_out[l].astype(BF16), _row(ln1_g[l]),
            _row(ln1_b[l]))
        stage = pltpu.VMEM((BATCH_TILE, LANE_BLOCKS, SUBLANES * RUN_PITCH, LANES), F32)
        x = _call(_mixer_kernel, "mixer", x, mixer_params, [
            stage,
            pltpu.VMEM((BATCH_TILE, (LRU_CONV - 1) * SUBLANES, D_MODEL), F32),
            pltpu.VMEM((BATCH_TILE, (SC_CONV - 1) * SUBLANES, D_MODEL), F32),
            pltpu.VMEM((BATCH_TILE, groups, SUBLANES, D_MODEL), F32),
            pltpu.VMEM((BATCH_TILE, groups, SUBLANES, D_MODEL), F32),
            pltpu.VMEM((BATCH_TILE, 1, D_MODEL), F32),
        ])(x, *mixer_params)

        xattn_ffn_params = (
            w_q[l].astype(BF16), w_k[l].astype(BF16), w_v[l].astype(BF16), w_xo[l].astype(BF16),
            _row(ln2_g[l]), _row(ln2_b[l]), w_ffn_gate[l].astype(BF16), w_ffn_up[l].astype(BF16),
            ffn_conv_w[l], _row(ffn_conv_b[l]), w_ffn_down[l].astype(BF16), _row(ln3_g[l]),
            _row(ln3_b[l]))
        mem_spec = pl.BlockSpec((BATCH_TILE, n_mem, D_MODEL), lambda b, t: (b, 0, 0))
        x = _call(_xattn_ffn_kernel, "xattn_convffn", x, xattn_ffn_params, [
            pltpu.VMEM((BATCH_TILE, n_mem, D_MODEL), BF16),
            pltpu.VMEM((BATCH_TILE, n_mem, D_MODEL), BF16),
            stage,
            pltpu.VMEM((BATCH_TILE, (FFN_CONV - 1) * SUBLANES, D_FF), F32),
        ], extra_specs=(mem_spec,))(x, mem, *xattn_ffn_params)
    return x
```

```python
import jax
import jax.numpy as jnp
from jax import lax
from jax.experimental import pallas as pl
from jax.experimental.pallas import tpu as pltpu

D_MODEL = 1024
N_LRU_HEADS = 8
LRU_BLOCK = D_MODEL // N_LRU_HEADS
LRU_CONV = 4
LRU_C = 8.0
SC_CONV = 3
N_XHEADS = 4
XHEAD_DIM = D_MODEL // N_XHEADS
D_FF = 3 * D_MODEL
FFN_CONV = 3
LN_EPS = 1e-5
DEPTH = 1
DN_ALPHA = (2.0 * DEPTH) ** 0.25

SUBLANES = 8
SEQ_TILE = 256
STEP_TILES = 2
LRU_CHUNKS = 4
LANES = 128
LANE_BLOCKS = D_MODEL // LANES
RUN = SEQ_TILE // SUBLANES
RUN_PITCH = RUN + SUBLANES
VMEM_LIMIT_BYTES = 60 * 1024 * 1024

F32 = jnp.float32
BF16 = jnp.bfloat16


def _dot(a, b):
    return jnp.dot(a, b, preferred_element_type=F32)


def _sigmoid(z):
    return 0.5 * jnp.tanh(0.5 * z) + 0.5


def _after(x, produced):
    bits = pltpu.bitcast(produced[0:1, :x.shape[-1]], jnp.uint32)
    return x + pltpu.bitcast((bits >> 16) >> 16, F32)


def _layer_norm(y, g, b):
    mu = jnp.mean(y, axis=-1, keepdims=True)
    yc = y - mu
    var = jnp.mean(yc * yc, axis=-1, keepdims=True)
    return yc * lax.rsqrt(var + LN_EPS) * g + b


def _run_interleaved(chains):
    chains = list(chains)
    while chains:
        for chain in list(chains):
            if next(chain, StopIteration) is StopIteration:
                chains.remove(chain)


def _to_interleaved(stage_ref, x):
    for c in range(LANE_BLOCKS):
        for i in range(SUBLANES):
            stage_ref[c, i * RUN_PITCH:i * RUN_PITCH + RUN, :] = (
                x[i * RUN:(i + 1) * RUN, c * LANES:(c + 1) * LANES])
    return jnp.concatenate(
        [jnp.concatenate([stage_ref[c, pl.ds(j, SUBLANES, stride=RUN_PITCH), :]
                          for c in range(LANE_BLOCKS)], axis=-1) for j in range(RUN)], axis=0)


def _from_interleaved(stage_ref, y):
    for j in range(RUN):
        for c in range(LANE_BLOCKS):
            stage_ref[c, pl.ds(j, SUBLANES, stride=RUN_PITCH), :] = (
                y[j * SUBLANES:(j + 1) * SUBLANES, c * LANES:(c + 1) * LANES])
    return jnp.concatenate(
        [jnp.concatenate([stage_ref[c, i * RUN_PITCH:i * RUN_PITCH + RUN, :]
                          for c in range(LANE_BLOCKS)], axis=-1) for i in range(SUBLANES)], axis=0)


def _interleaved_conv(tail_ref, cur, w_ref, width):
    rows, d = cur.shape
    taps = width - 1
    tail = cur[rows - taps * SUBLANES:]
    sub = lax.broadcasted_iota(jnp.int32, (taps, SUBLANES, d), 1)
    before = jnp.where(sub == 0,
                       pltpu.roll(tail_ref[...].reshape(taps, SUBLANES, d), 1, axis=1),
                       pltpu.roll(tail.reshape(taps, SUBLANES, d), 1, axis=1))
    before = before.reshape(taps * SUBLANES, d)
    tail_ref[...] = tail
    y = cur * w_ref[0:1, :]
    for k in range(1, width):
        shifted = jnp.concatenate(
            [before[(taps - k) * SUBLANES:], cur[:rows - k * SUBLANES]], axis=0)
        y = y + shifted * w_ref[k:k + 1, :]
    return y


def _mixer_kernel(x_ref, w_in_ref, cw_ref, cb_ref, w_ri_ref, b_r_ref, b_i_ref, lam_ref,
                  w_lru_out_ref, scw_ref, w_sc_out_ref, b_merge_ref, w_mix_ref, g_ref, b_ref,
                  o_ref, stage, lin_tail, sc_tail, hl_buf, ac_buf, h_carry):
    rows = SEQ_TILE
    crows = rows // LRU_CHUNKS
    cgroups = crows // SUBLANES
    t = pl.program_id(1)

    @pl.when(t == 0)
    def _():
        lin_tail[...] = jnp.zeros_like(lin_tail)
        sc_tail[...] = jnp.zeros_like(sc_tail)
        h_carry[...] = jnp.zeros_like(h_carry)

    nlam = -lam_ref[...]
    softplus = jnp.maximum(nlam, 0.0) + jnp.log1p(jnp.exp(-jnp.abs(nlam)))
    log_a_scale = -LRU_C * softplus

    def chain(s):
        x = _to_interleaved(stage.at[s], x_ref[s * rows:(s + 1) * rows, :])
        xb = x.astype(BF16)

        def seg(i):
            return _dot(xb, w_in_ref[:, i * D_MODEL:(i + 1) * D_MODEL])

        lin = seg(1)
        yield
        u = _interleaved_conv(lin_tail, lin, cw_ref, LRU_CONV) + cb_ref[...]
        ub = u.astype(BF16)
        yield
        z = [_dot(ub[:, h * LRU_BLOCK:(h + 1) * LRU_BLOCK], w_ri_ref[h])
             for h in range(N_LRU_HEADS)]
        zr = jnp.concatenate([zh[:, :LRU_BLOCK] for zh in z], axis=-1) + b_r_ref[...]
        zi = jnp.concatenate([zh[:, LRU_BLOCK:] for zh in z], axis=-1) + b_i_ref[...]
        sc_c = seg(3)
        yield

        def lru_chunk(c, state, produced):
            rs = slice(c * crows, (c + 1) * crows)
            log_a = _after(log_a_scale, produced) * _sigmoid(zr[rs])
            a = jnp.exp(log_a)
            th = jnp.tanh(log_a)
            gain = -2.0 * th / (1.0 - th)
            mult = jnp.where(gain > 0.0, gain * lax.rsqrt(gain), 0.0)
            if c == 0 and s == 0:
                row = lax.broadcasted_iota(jnp.int32, (crows, D_MODEL), 0)
                mult = jnp.where((row == 0) & (t == 0), 1.0, mult)
            bterm = mult * (_sigmoid(zi[rs]) * u[rs])
            a3 = a.reshape(cgroups, SUBLANES, D_MODEL)
            b3 = bterm.reshape(cgroups, SUBLANES, D_MODEL)
            for g in range(cgroups):
                if state is None:
                    hl, ac = b3[g], a3[g]
                else:
                    hl, ac = a3[g] * state[0] + b3[g], a3[g] * state[1]
                hl_buf[s, c * cgroups + g] = hl
                ac_buf[s, c * cgroups + g] = ac
                state = (hl, ac)
            return state

        state = lru_chunk(0, None, sc_c)
        yield
        sc_h = seg(4)
        sc_b = seg(2)
        yield
        state = lru_chunk(1, state, sc_b)
        conv_ch = _interleaved_conv(sc_tail, sc_c * sc_h, scw_ref, SC_CONV)
        sc_in = (sc_b * conv_ch).astype(BF16)
        yield
        y_sc = _dot(sc_in, w_sc_out_ref[...])
        g_sc = seg(6)
        yield
        state = lru_chunk(2, state, g_sc)
        yield
        lru_gate = seg(0)
        yield
        q, p = lru_chunk(3, state, lru_gate)
        sub = lax.broadcasted_iota(jnp.int32, (SUBLANES, D_MODEL), 0)
        for k in (1, 2, 4):
            keep = sub >= k
            q = jnp.where(keep, q + p * pltpu.roll(q, k, axis=0), q)
            p = jnp.where(keep, p * pltpu.roll(p, k, axis=0), p)
        h0 = h_carry[...]
        after = p * h0 + q
        entering = jnp.where(sub == 0, h0, pltpu.roll(after, 1, axis=0))
        h_carry[...] = after[SUBLANES - 1:SUBLANES, :]
        h = (hl_buf[s] + ac_buf[s] * entering).reshape(rows, D_MODEL)
        lru_in = (jax.nn.gelu(lru_gate, approximate=True) * h).astype(BF16)
        yield
        g_lru = seg(5)
        y_lru = _dot(lru_in, w_lru_out_ref[...])
        yield
        merged = (_sigmoid(g_lru + b_merge_ref[0:1, :]) * y_lru
                  + _sigmoid(g_sc + b_merge_ref[1:2, :]) * y_sc).astype(BF16)
        yield
        y = DN_ALPHA * x + _dot(merged, w_mix_ref[...])
        yield
        o_ref[s * rows:(s + 1) * rows, :] = _layer_norm(y, g_ref[...], b_ref[...])

    _run_interleaved(chain(s) for s in range(STEP_TILES))


def _xattn_ffn_kernel(x_ref, mem_ref, w_q_ref, w_k_ref, w_v_ref, w_xo_ref, g2_ref, b2_ref,
                      w_gate_ref, w_up_ref, cw_ref, cb_ref, w_down_ref, g3_ref, b3_ref,
                      o_ref, k_scr, v_scr, stage, gate_tail):
    t = pl.program_id(1)
    n_col = D_FF // D_MODEL
    cols = [slice(j * D_MODEL, (j + 1) * D_MODEL) for j in range(n_col)]
    head_cols = [slice(h * XHEAD_DIM, (h + 1) * XHEAD_DIM) for h in range(N_XHEADS)]

    @pl.when(t == 0)
    def _():
        gate_tail[...] = jnp.zeros_like(gate_tail)
        mb = mem_ref[...].astype(BF16)
        k_scr[...] = _dot(mb, w_k_ref[...]).astype(BF16)
        v_scr[...] = _dot(mb, w_v_ref[...]).astype(BF16)

    def chain(s):
        x = x_ref[s * SEQ_TILE:(s + 1) * SEQ_TILE, :]
        q = _dot(x.astype(BF16), w_q_ref[...])
        yield
        qb = (q * (XHEAD_DIM ** -0.5)).astype(BF16)
        yield
        scores = [lax.dot_general(qb[:, sl], k_scr[:, sl], (((1,), (1,)), ((), ())),
                                  preferred_element_type=F32) for sl in head_cols]
        yield
        probs, denoms = [], []
        for sc in scores:
            p = jnp.exp(sc - jnp.max(sc, axis=-1, keepdims=True))
            denoms.append(jnp.sum(p, axis=-1, keepdims=True))
            probs.append(p.astype(BF16))
        yield
        pv = [_dot(p, v_scr[:, sl]) for p, sl in zip(probs, head_cols)]
        yield
        att = jnp.concatenate([o / d for o, d in zip(pv, denoms)], axis=-1).astype(BF16)
        yield
        y = DN_ALPHA * x + _dot(att, w_xo_ref[...])
        yield
        x = _layer_norm(y, g2_ref[...], b2_ref[...])
        xb = x.astype(BF16)
        yield

        def gate_act(j, pre):
            sl = cols[j]
            gate = _interleaved_conv(gate_tail.at[:, sl], pre, cw_ref.at[:, sl], FFN_CONV)
            return jax.nn.gelu(gate + cb_ref[:, sl], approximate=True)

        acc = DN_ALPHA * x
        pre = _dot(xb, w_gate_ref[:, cols[0]])
        yield
        for j in range(n_col):
            act = gate_act(j, pre)
            yield
            up = _dot(xb, w_up_ref[:, cols[j]])
            if j + 1 < n_col:
                pre = _dot(xb, w_gate_ref[:, cols[j + 1]])
            yield
            hidden = (act * up).astype(BF16)
            yield
            acc = acc + _dot(hidden, w_down_ref[cols[j], :])
        yield
        o_ref[s * SEQ_TILE:(s + 1) * SEQ_TILE, :] = _from_interleaved(
            stage.at[s], _layer_norm(acc, g3_ref[...], b3_ref[...]))

    _run_interleaved(chain(s) for s in range(STEP_TILES))


def _resident(shape):
    zeros = (0,) * len(shape)
    return pl.BlockSpec(shape, lambda b, t: zeros, pipeline_mode=pl.Buffered(1))


def _seq_tiled(rows, cols):
    return pl.BlockSpec((None, rows, cols), lambda b, t: (b, t, 0))


def _call(body, name, x, inputs, scratch_shapes, extra_specs=(), to_bf16=()):
    bsz, seq, d = x.shape
    step_rows = STEP_TILES * SEQ_TILE
    assert seq % step_rows == 0 and d == D_MODEL
    steps_per_batch = seq // step_rows
    n_steps = bsz * steps_per_batch
    n_in = 1 + len(inputs)
    n_cast = len(to_bf16)

    def slab_spec(w):
        assert w.shape[0] % n_steps == 0
        return pl.BlockSpec((w.shape[0] // n_steps, w.shape[1]),
                            lambda b, t: (b * steps_per_batch + t, 0))

    def full_body(*refs):
        cast_in = refs[n_in:n_in + n_cast]
        cast_out = refs[n_in + n_cast + 1:n_in + 2 * n_cast + 1]
        for src, dst in zip(cast_in, cast_out):
            dst[...] = src[...].astype(BF16)
        body(*refs[:n_in], refs[n_in + n_cast], *refs[n_in + 2 * n_cast + 1:])

    resident = inputs[len(extra_specs):]
    return pl.pallas_call(
        full_body,
        name=name,
        grid=(bsz, steps_per_batch),
        in_specs=[_seq_tiled(step_rows, d), *extra_specs, *[_resident(p.shape) for p in resident],
                  *[slab_spec(w) for w in to_bf16]],
        out_specs=[_seq_tiled(step_rows, d), *[slab_spec(w) for w in to_bf16]],
        out_shape=[jax.ShapeDtypeStruct(x.shape, x.dtype),
                   *[jax.ShapeDtypeStruct(w.shape, BF16) for w in to_bf16]],
        scratch_shapes=scratch_shapes,
        compiler_params=pltpu.CompilerParams(
            dimension_semantics=("arbitrary", "arbitrary"),
            vmem_limit_bytes=VMEM_LIMIT_BYTES),
    )(x, *inputs, *to_bf16)


def _row(v):
    return v.reshape(1, -1)


def kernel(x, mem, w_in, lru_conv_w, lru_conv_b, w_rgate, b_rgate, w_igate, b_igate, lru_lambda,
           w_lru_out, sc_conv_w, w_sc_out, b_merge, w_mix_out, ln1_g, ln1_b, w_q, w_k, w_v, w_xo,
           ln2_g, ln2_b, w_ffn_gate, w_ffn_up, ffn_conv_w, ffn_conv_b, w_ffn_down, ln3_g, ln3_b):
    n_mem = mem.shape[1]
    groups = SEQ_TILE // SUBLANES
    for l in range(DEPTH):
        w_ri = jnp.concatenate([w_rgate[l], w_igate[l]], axis=-1).astype(BF16)
        mixer_params = (
            w_in[l].astype(BF16), lru_conv_w[l], _row(lru_conv_b[l]), w_ri, _row(b_rgate[l]),
            _row(b_igate[l]), _row(lru_lambda[l]), w_lru_out[l].astype(BF16), sc_conv_w[l],
            w_sc_out[l].astype(BF16), b_merge[l], w_mix_out[l].astype(BF16), _row(ln1_g[l]),
            _row(ln1_b[l]))
        stage = pltpu.VMEM((STEP_TILES, LANE_BLOCKS, SUBLANES * RUN_PITCH, LANES), F32)
        next_weights = (w_q[l], w_k[l], w_v[l], w_xo[l], w_ffn_gate[l], w_ffn_up[l], w_ffn_down[l])
        x, wq, wk, wv, wxo, wgate, wup, wdown = _call(_mixer_kernel, "mixer", x, mixer_params, [
            stage,
            pltpu.VMEM(((LRU_CONV - 1) * SUBLANES, D_MODEL), F32),
            pltpu.VMEM(((SC_CONV - 1) * SUBLANES, D_MODEL), F32),
            pltpu.VMEM((STEP_TILES, groups, SUBLANES, D_MODEL), F32),
            pltpu.VMEM((STEP_TILES, groups, SUBLANES, D_MODEL), F32),
            pltpu.VMEM((1, D_MODEL), F32),
        ], to_bf16=next_weights)

        xattn_ffn_inputs = (
            mem, wq, wk, wv, wxo, _row(ln2_g[l]), _row(ln2_b[l]), wgate, wup, ffn_conv_w[l],
            _row(ffn_conv_b[l]), wdown, _row(ln3_g[l]), _row(ln3_b[l]))
        mem_spec = pl.BlockSpec((None, n_mem, D_MODEL), lambda b, t: (b, 0, 0))
        x, = _call(_xattn_ffn_kernel, "xattn_convffn", x, xattn_ffn_inputs, [
            pltpu.VMEM((n_mem, D_MODEL), BF16),
            pltpu.VMEM((n_mem, D_MODEL), BF16),
            stage,
            pltpu.VMEM(((FFN_CONV - 1) * SUBLANES, D_FF), F32),
        ], extra_specs=(mem_spec,))
    return x
```

```python
import jax
import jax.numpy as jnp
from jax import lax
from jax.experimental import pallas as pl
from jax.experimental.pallas import tpu as pltpu

D_MODEL = 1024
N_LRU_HEADS = 8
LRU_BLOCK = D_MODEL // N_LRU_HEADS
LRU_CONV = 4
LRU_C = 8.0
SC_CONV = 3
N_XHEADS = 4
XHEAD_DIM = D_MODEL // N_XHEADS
D_FF = 3 * D_MODEL
FFN_CONV = 3
LN_EPS = 1e-5
DEPTH = 1
DN_ALPHA = (2.0 * DEPTH) ** 0.25

SUBLANES = 8
SEQ_TILE = 256
STEP_TILES = 2
LRU_CHUNKS = 4
LANES = 128
LANE_BLOCKS = D_MODEL // LANES
RUN = SEQ_TILE // SUBLANES
RUN_PITCH = RUN + SUBLANES
VMEM_LIMIT_BYTES = 60 * 1024 * 1024

F32 = jnp.float32
BF16 = jnp.bfloat16


def _dot(a, b):
    return jnp.dot(a, b, preferred_element_type=F32)


def _sigmoid(z):
    return 0.5 * jnp.tanh(0.5 * z) + 0.5


def _after(x, produced):
    bits = pltpu.bitcast(produced[0:1, :x.shape[-1]], jnp.uint32)
    return x + pltpu.bitcast((bits >> 16) >> 16, F32)


def _layer_norm(y, g, b):
    mu = jnp.mean(y, axis=-1, keepdims=True)
    yc = y - mu
    var = jnp.mean(yc * yc, axis=-1, keepdims=True)
    return yc * lax.rsqrt(var + LN_EPS) * g + b


def _run_interleaved(chains):
    chains = list(chains)
    while chains:
        for chain in list(chains):
            if next(chain, StopIteration) is StopIteration:
                chains.remove(chain)


def _to_interleaved(stage_ref, x):
    for c in range(LANE_BLOCKS):
        for i in range(SUBLANES):
            stage_ref[c, i * RUN_PITCH:i * RUN_PITCH + RUN, :] = (
                x[i * RUN:(i + 1) * RUN, c * LANES:(c + 1) * LANES])
    return jnp.concatenate(
        [jnp.concatenate([stage_ref[c, pl.ds(j, SUBLANES, stride=RUN_PITCH), :]
                          for c in range(LANE_BLOCKS)], axis=-1) for j in range(RUN)], axis=0)


def _from_interleaved(stage_ref, y):
    for j in range(RUN):
        for c in range(LANE_BLOCKS):
            stage_ref[c, pl.ds(j, SUBLANES, stride=RUN_PITCH), :] = (
                y[j * SUBLANES:(j + 1) * SUBLANES, c * LANES:(c + 1) * LANES])
    return jnp.concatenate(
        [jnp.concatenate([stage_ref[c, i * RUN_PITCH:i * RUN_PITCH + RUN, :]
                          for c in range(LANE_BLOCKS)], axis=-1) for i in range(SUBLANES)], axis=0)


def _interleaved_conv(tail_ref, cur, w_ref, width):
    rows, d = cur.shape
    taps = width - 1
    tail = cur[rows - taps * SUBLANES:]
    sub = lax.broadcasted_iota(jnp.int32, (taps, SUBLANES, d), 1)
    before = jnp.where(sub == 0,
                       pltpu.roll(tail_ref[...].reshape(taps, SUBLANES, d), 1, axis=1),
                       pltpu.roll(tail.reshape(taps, SUBLANES, d), 1, axis=1))
    before = before.reshape(taps * SUBLANES, d)
    tail_ref[...] = tail
    y = cur * w_ref[0:1, :]
    for k in range(1, width):
        shifted = jnp.concatenate(
            [before[(taps - k) * SUBLANES:], cur[:rows - k * SUBLANES]], axis=0)
        y = y + shifted * w_ref[k:k + 1, :]
    return y


def _mixer_kernel(x_ref, w_in_ref, cw_ref, cb_ref, w_ri_ref, b_r_ref, b_i_ref, lam_ref,
                  w_lru_out_ref, scw_ref, w_sc_out_ref, b_merge_ref, w_mix_ref, g_ref, b_ref,
                  o_ref, stage, lin_tail, sc_tail, hl_buf, ac_buf, h_carry):
    rows = SEQ_TILE
    crows = rows // LRU_CHUNKS
    cgroups = crows // SUBLANES
    t = pl.program_id(1)

    @pl.when(t == 0)
    def _():
        lin_tail[...] = jnp.zeros_like(lin_tail)
        sc_tail[...] = jnp.zeros_like(sc_tail)
        h_carry[...] = jnp.zeros_like(h_carry)

    nlam = -lam_ref[...]
    softplus = jnp.maximum(nlam, 0.0) + jnp.log1p(jnp.exp(-jnp.abs(nlam)))
    log_a_scale = -LRU_C * softplus

    def chain(s):
        x = _to_interleaved(stage.at[s], x_ref[s * rows:(s + 1) * rows, :])
        xb = x.astype(BF16)

        def seg(i):
            return _dot(xb, w_in_ref[:, i * D_MODEL:(i + 1) * D_MODEL])

        lin = seg(1)
        yield
        u = _interleaved_conv(lin_tail, lin, cw_ref, LRU_CONV) + cb_ref[...]
        ub = u.astype(BF16)
        yield
        z = [_dot(ub[:, h * LRU_BLOCK:(h + 1) * LRU_BLOCK], w_ri_ref[h])
             for h in range(N_LRU_HEADS)]
        zr = jnp.concatenate([zh[:, :LRU_BLOCK] for zh in z], axis=-1) + b_r_ref[...]
        zi = jnp.concatenate([zh[:, LRU_BLOCK:] for zh in z], axis=-1) + b_i_ref[...]
        sc_c = seg(3)
        yield

        def lru_chunk(c, state, produced):
            rs = slice(c * crows, (c + 1) * crows)
            log_a = _after(log_a_scale, produced) * _sigmoid(zr[rs])
            a = jnp.exp(log_a)
            th = jnp.tanh(log_a)
            gain = -2.0 * th / (1.0 - th)
            mult = jnp.where(gain > 0.0, gain * lax.rsqrt(gain), 0.0)
            if c == 0 and s == 0:
                row = lax.broadcasted_iota(jnp.int32, (crows, D_MODEL), 0)
                mult = jnp.where((row == 0) & (t == 0), 1.0, mult)
            bterm = mult * (_sigmoid(zi[rs]) * u[rs])
            a3 = a.reshape(cgroups, SUBLANES, D_MODEL)
            b3 = bterm.reshape(cgroups, SUBLANES, D_MODEL)
            for g in range(cgroups):
                if state is None:
                    hl, ac = b3[g], a3[g]
                else:
                    hl, ac = a3[g] * state[0] + b3[g], a3[g] * state[1]
                hl_buf[s, c * cgroups + g] = hl
                ac_buf[s, c * cgroups + g] = ac
                state = (hl, ac)
            return state

        state = lru_chunk(0, None, sc_c)
        yield
        sc_h = seg(4)
        sc_b = seg(2)
        yield
        state = lru_chunk(1, state, sc_b)
        conv_ch = _interleaved_conv(sc_tail, sc_c * sc_h, scw_ref, SC_CONV)
        sc_in = (sc_b * conv_ch).astype(BF16)
        yield
        y_sc = _dot(sc_in, w_sc_out_ref[...])
        g_sc = seg(6)
        yield
        state = lru_chunk(2, state, g_sc)
        yield
        lru_gate = seg(0)
        yield
        q, p = lru_chunk(3, state, lru_gate)
        sub = lax.broadcasted_iota(jnp.int32, (SUBLANES, D_MODEL), 0)
        for k in (1, 2, 4):
            keep = sub >= k
            q = jnp.where(keep, q + p * pltpu.roll(q, k, axis=0), q)
            p = jnp.where(keep, p * pltpu.roll(p, k, axis=0), p)
        h0 = h_carry[...]
        after = p * h0 + q
        entering = jnp.where(sub == 0, h0, pltpu.roll(after, 1, axis=0))
        h_carry[...] = after[SUBLANES - 1:SUBLANES, :]
        h = (hl_buf[s] + ac_buf[s] * entering).reshape(rows, D_MODEL)
        lru_in = (jax.nn.gelu(lru_gate, approximate=True) * h).astype(BF16)
        yield
        g_lru = seg(5)
        y_lru = _dot(lru_in, w_lru_out_ref[...])
        yield
        merged = (_sigmoid(g_lru + b_merge_ref[0:1, :]) * y_lru
                  + _sigmoid(g_sc + b_merge_ref[1:2, :]) * y_sc).astype(BF16)
        yield
        y = DN_ALPHA * x + _dot(merged, w_mix_ref[...])
        yield
        o_ref[s * rows:(s + 1) * rows, :] = _layer_norm(y, g_ref[...], b_ref[...])

    _run_interleaved(chain(s) for s in range(STEP_TILES))


def _xattn_ffn_kernel(x_ref, mem_ref, w_q_ref, w_k_ref, w_v_ref, w_xo_ref, g2_ref, b2_ref,
                      w_gate_ref, w_up_ref, cw_ref, cb_ref, w_down_ref, g3_ref, b3_ref,
                      o_ref, k_scr, v_scr, stage, gate_tail):
    t = pl.program_id(1)
    n_col = D_FF // D_MODEL
    cols = [slice(j * D_MODEL, (j + 1) * D_MODEL) for j in range(n_col)]
    head_cols = [slice(h * XHEAD_DIM, (h + 1) * XHEAD_DIM) for h in range(N_XHEADS)]

    @pl.when(t == 0)
    def _():
        gate_tail[...] = jnp.zeros_like(gate_tail)
        mb = mem_ref[...].astype(BF16)
        k_scr[...] = _dot(mb, w_k_ref[...]).astype(BF16)
        v_scr[...] = _dot(mb, w_v_ref[...]).astype(BF16)

    def chain(s):
        x = x_ref[s * SEQ_TILE:(s + 1) * SEQ_TILE, :]
        q = _dot(x.astype(BF16), w_q_ref[...])
        yield
        qb = (q * (XHEAD_DIM ** -0.5)).astype(BF16)
        yield
        scores = [lax.dot_general(qb[:, sl], k_scr[:, sl], (((1,), (1,)), ((), ())),
                                  preferred_element_type=F32) for sl in head_cols]
        yield
        probs, denoms = [], []
        for sc in scores:
            p = jnp.exp(sc - jnp.max(sc, axis=-1, keepdims=True))
            denoms.append(jnp.sum(p, axis=-1, keepdims=True))
            probs.append(p.astype(BF16))
        yield
        pv = [_dot(p, v_scr[:, sl]) for p, sl in zip(probs, head_cols)]
        yield
        att = jnp.concatenate([o / d for o, d in zip(pv, denoms)], axis=-1).astype(BF16)
        yield
        y = DN_ALPHA * x + _dot(att, w_xo_ref[...])
        yield
        x = _layer_norm(y, g2_ref[...], b2_ref[...])
        xb = x.astype(BF16)
        yield

        def gate_act(j, pre):
            sl = cols[j]
            gate = _interleaved_conv(gate_tail.at[:, sl], pre, cw_ref.at[:, sl], FFN_CONV)
            return jax.nn.gelu(gate + cb_ref[:, sl], approximate=True)

        acc = DN_ALPHA * x
        pre = _dot(xb, w_gate_ref[:, cols[0]])
        up = _dot(xb, w_up_ref[:, cols[0]])
        yield
        for j in range(n_col):
            this_pre, this_up = pre, up
            if j + 1 < n_col:
                pre = _dot(xb, w_gate_ref[:, cols[j + 1]])
                up = _dot(xb, w_up_ref[:, cols[j + 1]])
            yield
            act = gate_act(j, this_pre)
            yield
            hidden = (act * this_up).astype(BF16)
            yield
            acc = acc + _dot(hidden, w_down_ref[cols[j], :])
        yield
        o_ref[s * SEQ_TILE:(s + 1) * SEQ_TILE, :] = _from_interleaved(
            stage.at[s], _layer_norm(acc, g3_ref[...], b3_ref[...]))

    _run_interleaved(chain(s) for s in range(STEP_TILES))


def _resident(shape):
    zeros = (0,) * len(shape)
    return pl.BlockSpec(shape, lambda b, t: zeros, pipeline_mode=pl.Buffered(1))


def _seq_tiled(rows, cols):
    return pl.BlockSpec((None, rows, cols), lambda b, t: (b, t, 0))


def _call(body, name, x, inputs, scratch_shapes, extra_specs=(), to_bf16=()):
    bsz, seq, d = x.shape
    step_rows = STEP_TILES * SEQ_TILE
    assert seq % step_rows == 0 and d == D_MODEL
    steps_per_batch = seq // step_rows
    n_steps = bsz * steps_per_batch
    n_in = 1 + len(inputs)
    n_cast = len(to_bf16)

    def slab_spec(w):
        assert w.shape[0] % n_steps == 0
        return pl.BlockSpec((w.shape[0] // n_steps, w.shape[1]),
                            lambda b, t: (b * steps_per_batch + t, 0))

    def full_body(*refs):
        cast_in = refs[n_in:n_in + n_cast]
        cast_out = refs[n_in + n_cast + 1:n_in + 2 * n_cast + 1]
        for src, dst in zip(cast_in, cast_out):
            dst[...] = src[...].astype(BF16)
        body(*refs[:n_in], refs[n_in + n_cast], *refs[n_in + 2 * n_cast + 1:])

    resident = inputs[len(extra_specs):]
    return pl.pallas_call(
        full_body,
        name=name,
        grid=(bsz, steps_per_batch),
        in_specs=[_seq_tiled(step_rows, d), *extra_specs, *[_resident(p.shape) for p in resident],
                  *[slab_spec(w) for w in to_bf16]],
        out_specs=[_seq_tiled(step_rows, d), *[slab_spec(w) for w in to_bf16]],
        out_shape=[jax.ShapeDtypeStruct(x.shape, x.dtype),
                   *[jax.ShapeDtypeStruct(w.shape, BF16) for w in to_bf16]],
        scratch_shapes=scratch_shapes,
        compiler_params=pltpu.CompilerParams(
            dimension_semantics=("arbitrary", "arbitrary"),
            vmem_limit_bytes=VMEM_LIMIT_BYTES),
    )(x, *inputs, *to_bf16)


def _row(v):
    return v.reshape(1, -1)


def kernel(x, mem, w_in, lru_conv_w, lru_conv_b, w_rgate, b_rgate, w_igate, b_igate, lru_lambda,
           w_lru_out, sc_conv_w, w_sc_out, b_merge, w_mix_out, ln1_g, ln1_b, w_q, w_k, w_v, w_xo,
           ln2_g, ln2_b, w_ffn_gate, w_ffn_up, ffn_conv_w, ffn_conv_b, w_ffn_down, ln3_g, ln3_b):
    n_mem = mem.shape[1]
    groups = SEQ_TILE // SUBLANES
    for l in range(DEPTH):
        w_ri = jnp.concatenate([w_rgate[l], w_igate[l]], axis=-1).astype(BF16)
        mixer_params = (
            w_in[l].astype(BF16), lru_conv_w[l], _row(lru_conv_b[l]), w_ri, _row(b_rgate[l]),
            _row(b_igate[l]), _row(lru_lambda[l]), w_lru_out[l].astype(BF16), sc_conv_w[l],
            w_sc_out[l].astype(BF16), b_merge[l], w_mix_out[l].astype(BF16), _row(ln1_g[l]),
            _row(ln1_b[l]))
        stage = pltpu.VMEM((STEP_TILES, LANE_BLOCKS, SUBLANES * RUN_PITCH, LANES), F32)
        next_weights = (w_q[l], w_k[l], w_v[l], w_xo[l], w_ffn_gate[l], w_ffn_up[l], w_ffn_down[l])
        x, wq, wk, wv, wxo, wgate, wup, wdown = _call(_mixer_kernel, "mixer", x, mixer_params, [
            stage,
            pltpu.VMEM(((LRU_CONV - 1) * SUBLANES, D_MODEL), F32),
            pltpu.VMEM(((SC_CONV - 1) * SUBLANES, D_MODEL), F32),
            pltpu.VMEM((STEP_TILES, groups, SUBLANES, D_MODEL), F32),
            pltpu.VMEM((STEP_TILES, groups, SUBLANES, D_MODEL), F32),
            pltpu.VMEM((1, D_MODEL), F32),
        ], to_bf16=next_weights)

        xattn_ffn_inputs = (
            mem, wq, wk, wv, wxo, _row(ln2_g[l]), _row(ln2_b[l]), wgate, wup, ffn_conv_w[l],
            _row(ffn_conv_b[l]), wdown, _row(ln3_g[l]), _row(ln3_b[l]))
        mem_spec = pl.BlockSpec((None, n_mem, D_MODEL), lambda b, t: (b, 0, 0))
        x, = _call(_xattn_ffn_kernel, "xattn_convffn", x, xattn_ffn_inputs, [
            pltpu.VMEM((n_mem, D_MODEL), BF16),
            pltpu.VMEM((n_mem, D_MODEL), BF16),
            stage,
            pltpu.VMEM(((FFN_CONV - 1) * SUBLANES, D_FF), F32),
        ], extra_specs=(mem_spec,))
    return x
```

```python
import jax
import jax.numpy as jnp
from jax import lax
from jax.experimental import pallas as pl
from jax.experimental.pallas import tpu as pltpu

D_MODEL = 1024
N_LRU_HEADS = 8
LRU_BLOCK = D_MODEL // N_LRU_HEADS
LRU_CONV = 4
LRU_C = 8.0
SC_CONV = 3
N_XHEADS = 4
XHEAD_DIM = D_MODEL // N_XHEADS
D_FF = 3 * D_MODEL
FFN_CONV = 3
LN_EPS = 1e-5
DEPTH = 1
DN_ALPHA = (2.0 * DEPTH) ** 0.25

SUBLANES = 8
SEQ_TILE = 256
STEP_TILES = 2
LRU_CHUNKS = 4
FFN_BLOCK = 512
LANES = 128
LANE_BLOCKS = D_MODEL // LANES
RUN = SEQ_TILE // SUBLANES
RUN_PITCH = RUN + SUBLANES
VMEM_LIMIT_BYTES = 60 * 1024 * 1024

F32 = jnp.float32
BF16 = jnp.bfloat16


def _dot(a, b):
    return jnp.dot(a, b, preferred_element_type=F32)


def _sigmoid(z):
    return 0.5 * jnp.tanh(0.5 * z) + 0.5


def _after(x, produced):
    bits = pltpu.bitcast(produced[0:1, :x.shape[-1]], jnp.uint32)
    return x + pltpu.bitcast((bits >> 16) >> 16, F32)


def _layer_norm(y, g, b):
    mu = jnp.mean(y, axis=-1, keepdims=True)
    yc = y - mu
    var = jnp.mean(yc * yc, axis=-1, keepdims=True)
    return yc * lax.rsqrt(var + LN_EPS) * g + b


def _run_interleaved(chains):
    chains = list(chains)
    while chains:
        for chain in list(chains):
            if next(chain, StopIteration) is StopIteration:
                chains.remove(chain)


def _to_interleaved(stage_ref, x):
    for c in range(LANE_BLOCKS):
        for i in range(SUBLANES):
            stage_ref[c, i * RUN_PITCH:i * RUN_PITCH + RUN, :] = (
                x[i * RUN:(i + 1) * RUN, c * LANES:(c + 1) * LANES])
    return jnp.concatenate(
        [jnp.concatenate([stage_ref[c, pl.ds(j, SUBLANES, stride=RUN_PITCH), :]
                          for c in range(LANE_BLOCKS)], axis=-1) for j in range(RUN)], axis=0)


def _from_interleaved(stage_ref, y):
    for j in range(RUN):
        for c in range(LANE_BLOCKS):
            stage_ref[c, pl.ds(j, SUBLANES, stride=RUN_PITCH), :] = (
                y[j * SUBLANES:(j + 1) * SUBLANES, c * LANES:(c + 1) * LANES])
    return jnp.concatenate(
        [jnp.concatenate([stage_ref[c, i * RUN_PITCH:i * RUN_PITCH + RUN, :]
                          for c in range(LANE_BLOCKS)], axis=-1) for i in range(SUBLANES)], axis=0)


def _interleaved_conv(tail_ref, cur, w_ref, width):
    rows, d = cur.shape
    taps = width - 1
    tail = cur[rows - taps * SUBLANES:]
    sub = lax.broadcasted_iota(jnp.int32, (taps, SUBLANES, d), 1)
    before = jnp.where(sub == 0,
                       pltpu.roll(tail_ref[...].reshape(taps, SUBLANES, d), 1, axis=1),
                       pltpu.roll(tail.reshape(taps, SUBLANES, d), 1, axis=1))
    before = before.reshape(taps * SUBLANES, d)
    tail_ref[...] = tail
    y = cur * w_ref[0:1, :]
    for k in range(1, width):
        shifted = jnp.concatenate(
            [before[(taps - k) * SUBLANES:], cur[:rows - k * SUBLANES]], axis=0)
        y = y + shifted * w_ref[k:k + 1, :]
    return y


def _mixer_kernel(x_ref, w_in_ref, cw_ref, cb_ref, w_ri_ref, b_r_ref, b_i_ref, lam_ref,
                  w_lru_out_ref, scw_ref, w_sc_out_ref, b_merge_ref, w_mix_ref, g_ref, b_ref,
                  o_ref, stage, lin_tail, sc_tail, hl_buf, ac_buf, h_carry):
    rows = SEQ_TILE
    crows = rows // LRU_CHUNKS
    cgroups = crows // SUBLANES
    t = pl.program_id(1)

    @pl.when(t == 0)
    def _():
        lin_tail[...] = jnp.zeros_like(lin_tail)
        sc_tail[...] = jnp.zeros_like(sc_tail)
        h_carry[...] = jnp.zeros_like(h_carry)

    nlam = -lam_ref[...]
    softplus = jnp.maximum(nlam, 0.0) + jnp.log1p(jnp.exp(-jnp.abs(nlam)))
    log_a_scale = -LRU_C * softplus

    def chain(s):
        x = _to_interleaved(stage.at[s], x_ref[s * rows:(s + 1) * rows, :])
        xb = x.astype(BF16)

        def seg(i):
            return _dot(xb, w_in_ref[:, i * D_MODEL:(i + 1) * D_MODEL])

        lin = seg(1)
        yield
        u = _interleaved_conv(lin_tail, lin, cw_ref, LRU_CONV) + cb_ref[...]
        ub = u.astype(BF16)
        yield
        z = [_dot(ub[:, h * LRU_BLOCK:(h + 1) * LRU_BLOCK], w_ri_ref[h])
             for h in range(N_LRU_HEADS)]
        zr = jnp.concatenate([zh[:, :LRU_BLOCK] for zh in z], axis=-1) + b_r_ref[...]
        zi = jnp.concatenate([zh[:, LRU_BLOCK:] for zh in z], axis=-1) + b_i_ref[...]
        sc_c = seg(3)
        yield

        def lru_chunk(c, state, produced):
            rs = slice(c * crows, (c + 1) * crows)
            log_a = _after(log_a_scale, produced) * _sigmoid(zr[rs])
            a = jnp.exp(log_a)
            th = jnp.tanh(log_a)
            gain = -2.0 * th / (1.0 - th)
            mult = jnp.where(gain > 0.0, gain * lax.rsqrt(gain), 0.0)
            if c == 0 and s == 0:
                row = lax.broadcasted_iota(jnp.int32, (crows, D_MODEL), 0)
                mult = jnp.where((row == 0) & (t == 0), 1.0, mult)
            bterm = mult * (_sigmoid(zi[rs]) * u[rs])
            a3 = a.reshape(cgroups, SUBLANES, D_MODEL)
            b3 = bterm.reshape(cgroups, SUBLANES, D_MODEL)
            for g in range(cgroups):
                if state is None:
                    hl, ac = b3[g], a3[g]
                else:
                    hl, ac = a3[g] * state[0] + b3[g], a3[g] * state[1]
                hl_buf[s, c * cgroups + g] = hl
                ac_buf[s, c * cgroups + g] = ac
                state = (hl, ac)
            return state

        state = lru_chunk(0, None, sc_c)
        yield
        sc_h = seg(4)
        sc_b = seg(2)
        yield
        state = lru_chunk(1, state, sc_b)
        conv_ch = _interleaved_conv(sc_tail, sc_c * sc_h, scw_ref, SC_CONV)
        sc_in = (sc_b * conv_ch).astype(BF16)
        yield
        y_sc = _dot(sc_in, w_sc_out_ref[...])
        g_sc = seg(6)
        yield
        state = lru_chunk(2, state, g_sc)
        yield
        lru_gate = seg(0)
        yield
        q, p = lru_chunk(3, state, lru_gate)
        sub = lax.broadcasted_iota(jnp.int32, (SUBLANES, D_MODEL), 0)
        for k in (1, 2, 4):
            keep = sub >= k
            q = jnp.where(keep, q + p * pltpu.roll(q, k, axis=0), q)
            p = jnp.where(keep, p * pltpu.roll(p, k, axis=0), p)
        h0 = h_carry[...]
        after = p * h0 + q
        entering = jnp.where(sub == 0, h0, pltpu.roll(after, 1, axis=0))
        h_carry[...] = after[SUBLANES - 1:SUBLANES, :]
        h = (hl_buf[s] + ac_buf[s] * entering).reshape(rows, D_MODEL)
        lru_in = (jax.nn.gelu(lru_gate, approximate=True) * h).astype(BF16)
        yield
        g_lru = seg(5)
        y_lru = _dot(lru_in, w_lru_out_ref[...])
        yield
        merged = (_sigmoid(g_lru + b_merge_ref[0:1, :]) * y_lru
                  + _sigmoid(g_sc + b_merge_ref[1:2, :]) * y_sc).astype(BF16)
        yield
        y = DN_ALPHA * x + _dot(merged, w_mix_ref[...])
        yield
        o_ref[s * rows:(s + 1) * rows, :] = _layer_norm(y, g_ref[...], b_ref[...])

    _run_interleaved(chain(s) for s in range(STEP_TILES))


def _xattn_ffn_kernel(x_ref, mem_ref, w_q_ref, w_k_ref, w_v_ref, w_xo_ref, g2_ref, b2_ref,
                      w_gate_ref, w_up_ref, cw_ref, cb_ref, w_down_ref, g3_ref, b3_ref,
                      o_ref, k_scr, v_scr, stage, gate_tail):
    t = pl.program_id(1)
    n_col = D_FF // FFN_BLOCK
    cols = [slice(j * FFN_BLOCK, (j + 1) * FFN_BLOCK) for j in range(n_col)]
    head_cols = [slice(h * XHEAD_DIM, (h + 1) * XHEAD_DIM) for h in range(N_XHEADS)]

    @pl.when(t == 0)
    def _():
        gate_tail[...] = jnp.zeros_like(gate_tail)
        mb = mem_ref[...].astype(BF16)
        k_scr[...] = _dot(mb, w_k_ref[...]).astype(BF16)
        v_scr[...] = _dot(mb, w_v_ref[...]).astype(BF16)

    def chain(s):
        x = x_ref[s * SEQ_TILE:(s + 1) * SEQ_TILE, :]
        q = _dot(x.astype(BF16), w_q_ref[...])
        yield
        qb = (q * (XHEAD_DIM ** -0.5)).astype(BF16)
        yield
        scores = [lax.dot_general(qb[:, sl], k_scr[:, sl], (((1,), (1,)), ((), ())),
                                  preferred_element_type=F32) for sl in head_cols]
        yield
        probs, denoms = [], []
        for sc in scores:
            p = jnp.exp(sc - jnp.max(sc, axis=-1, keepdims=True))
            denoms.append(jnp.sum(p, axis=-1, keepdims=True))
            probs.append(p.astype(BF16))
        yield
        pv = [_dot(p, v_scr[:, sl]) for p, sl in zip(probs, head_cols)]
        yield
        att = jnp.concatenate([o / d for o, d in zip(pv, denoms)], axis=-1).astype(BF16)
        yield
        y = DN_ALPHA * x + _dot(att, w_xo_ref[...])
        yield
        x = _layer_norm(y, g2_ref[...], b2_ref[...])
        xb = x.astype(BF16)
        yield

        def gate_act(j, pre):
            sl = cols[j]
            gate = _interleaved_conv(gate_tail.at[:, sl], pre, cw_ref.at[:, sl], FFN_CONV)
            return jax.nn.gelu(gate + cb_ref[:, sl], approximate=True)

        acc = DN_ALPHA * x
        pre = _dot(xb, w_gate_ref[:, cols[0]])
        up = _dot(xb, w_up_ref[:, cols[0]])
        yield
        for j in range(n_col):
            this_pre, this_up = pre, up
            if j + 1 < n_col:
                pre = _dot(xb, w_gate_ref[:, cols[j + 1]])
                up = _dot(xb, w_up_ref[:, cols[j + 1]])
            yield
            act = gate_act(j, this_pre)
            yield
            hidden = (act * this_up).astype(BF16)
            yield
            acc = acc + _dot(hidden, w_down_ref[cols[j], :])
        yield
        o_ref[s * SEQ_TILE:(s + 1) * SEQ_TILE, :] = _from_interleaved(
            stage.at[s], _layer_norm(acc, g3_ref[...], b3_ref[...]))

    _run_interleaved(chain(s) for s in range(STEP_TILES))


def _resident(shape):
    zeros = (0,) * len(shape)
    return pl.BlockSpec(shape, lambda b, t: zeros, pipeline_mode=pl.Buffered(1))


def _seq_tiled(rows, cols):
    return pl.BlockSpec((None, rows, cols), lambda b, t: (b, t, 0))


def _call(body, name, x, inputs, scratch_shapes, extra_specs=(), to_bf16=()):
    bsz, seq, d = x.shape
    step_rows = STEP_TILES * SEQ_TILE
    assert seq % step_rows == 0 and d == D_MODEL
    steps_per_batch = seq // step_rows
    n_steps = bsz * steps_per_batch
    n_in = 1 + len(inputs)
    n_cast = len(to_bf16)

    def slab_spec(w):
        assert w.shape[0] % n_steps == 0
        return pl.BlockSpec((w.shape[0] // n_steps, w.shape[1]),
                            lambda b, t: (b * steps_per_batch + t, 0))

    def full_body(*refs):
        cast_in = refs[n_in:n_in + n_cast]
        cast_out = refs[n_in + n_cast + 1:n_in + 2 * n_cast + 1]
        for src, dst in zip(cast_in, cast_out):
            dst[...] = src[...].astype(BF16)
        body(*refs[:n_in], refs[n_in + n_cast], *refs[n_in + 2 * n_cast + 1:])

    resident = inputs[len(extra_specs):]
    return pl.pallas_call(
        full_body,
        name=name,
        grid=(bsz, steps_per_batch),
        in_specs=[_seq_tiled(step_rows, d), *extra_specs, *[_resident(p.shape) for p in resident],
                  *[slab_spec(w) for w in to_bf16]],
        out_specs=[_seq_tiled(step_rows, d), *[slab_spec(w) for w in to_bf16]],
        out_shape=[jax.ShapeDtypeStruct(x.shape, x.dtype),
                   *[jax.ShapeDtypeStruct(w.shape, BF16) for w in to_bf16]],
        scratch_shapes=scratch_shapes,
        compiler_params=pltpu.CompilerParams(
            dimension_semantics=("arbitrary", "arbitrary"),
            vmem_limit_bytes=VMEM_LIMIT_BYTES),
    )(x, *inputs, *to_bf16)


def _row(v):
    return v.reshape(1, -1)


def kernel(x, mem, w_in, lru_conv_w, lru_conv_b, w_rgate, b_rgate, w_igate, b_igate, lru_lambda,
           w_lru_out, sc_conv_w, w_sc_out, b_merge, w_mix_out, ln1_g, ln1_b, w_q, w_k, w_v, w_xo,
           ln2_g, ln2_b, w_ffn_gate, w_ffn_up, ffn_conv_w, ffn_conv_b, w_ffn_down, ln3_g, ln3_b):
    n_mem = mem.shape[1]
    groups = SEQ_TILE // SUBLANES
    for l in range(DEPTH):
        w_ri = jnp.concatenate([w_rgate[l], w_igate[l]], axis=-1).astype(BF16)
        mixer_params = (
            w_in[l].astype(BF16), lru_conv_w[l], _row(lru_conv_b[l]), w_ri, _row(b_rgate[l]),
            _row(b_igate[l]), _row(lru_lambda[l]), w_lru_out[l].astype(BF16), sc_conv_w[l],
            w_sc_out[l].astype(BF16), b_merge[l], w_mix_out[l].astype(BF16), _row(ln1_g[l]),
            _row(ln1_b[l]))
        stage = pltpu.VMEM((STEP_TILES, LANE_BLOCKS, SUBLANES * RUN_PITCH, LANES), F32)
        next_weights = (w_q[l], w_k[l], w_v[l], w_xo[l], w_ffn_gate[l], w_ffn_up[l], w_ffn_down[l])
        x, wq, wk, wv, wxo, wgate, wup, wdown = _call(_mixer_kernel, "mixer", x, mixer_params, [
            stage,
            pltpu.VMEM(((LRU_CONV - 1) * SUBLANES, D_MODEL), F32),
            pltpu.VMEM(((SC_CONV - 1) * SUBLANES, D_MODEL), F32),
            pltpu.VMEM((STEP_TILES, groups, SUBLANES, D_MODEL), F32),
            pltpu.VMEM((STEP_TILES, groups, SUBLANES, D_MODEL), F32),
            pltpu.VMEM((1, D_MODEL), F32),
        ], to_bf16=next_weights)

        xattn_ffn_inputs = (
            mem, wq, wk, wv, wxo, _row(ln2_g[l]), _row(ln2_b[l]), wgate, wup, ffn_conv_w[l],
            _row(ffn_conv_b[l]), wdown, _row(ln3_g[l]), _row(ln3_b[l]))
        mem_spec = pl.BlockSpec((None, n_mem, D_MODEL), lambda b, t: (b, 0, 0))
        x, = _call(_xattn_ffn_kernel, "xattn_convffn", x, xattn_ffn_inputs, [
            pltpu.VMEM((n_mem, D_MODEL), BF16),
            pltpu.VMEM((n_mem, D_MODEL), BF16),
            stage,
            pltpu.VMEM(((FFN_CONV - 1) * SUBLANES, D_FF), F32),
        ], extra_specs=(mem_spec,))
    return x
```

```python
import jax
import jax.numpy as jnp
from jax import lax
from jax.experimental import pallas as pl
from jax.experimental.pallas import tpu as pltpu

D_MODEL = 1024
N_LRU_HEADS = 8
LRU_BLOCK = D_MODEL // N_LRU_HEADS
LRU_CONV = 4
LRU_C = 8.0
SC_CONV = 3
N_XHEADS = 4
XHEAD_DIM = D_MODEL // N_XHEADS
D_FF = 3 * D_MODEL
FFN_CONV = 3
LN_EPS = 1e-5
DEPTH = 1
DN_ALPHA = (2.0 * DEPTH) ** 0.25

SUBLANES = 8
SEQ_TILE = 256
STEP_TILES = 2
LRU_CHUNKS = 4
LN_CHUNKS = 4
LANES = 128
LANE_BLOCKS = D_MODEL // LANES
RUN = SEQ_TILE // SUBLANES
RUN_PITCH = RUN + SUBLANES
VMEM_LIMIT_BYTES = 60 * 1024 * 1024

F32 = jnp.float32
BF16 = jnp.bfloat16


def _dot(a, b):
    return jnp.dot(a, b, preferred_element_type=F32)


def _sigmoid(z):
    return 0.5 * jnp.tanh(0.5 * z) + 0.5


def _after(x, produced):
    bits = pltpu.bitcast(produced[0:1, :x.shape[-1]], jnp.uint32)
    return x + pltpu.bitcast((bits >> 16) >> 16, F32)


def _layer_norm_rows(y, g, b):
    mu = jnp.mean(y, axis=-1, keepdims=True)
    yc = y - mu
    var = jnp.mean(yc * yc, axis=-1, keepdims=True)
    return yc * lax.rsqrt(var + LN_EPS) * g + b


def _layer_norm(y, g, b):
    rows = y.shape[0] // LN_CHUNKS
    return jnp.concatenate(
        [_layer_norm_rows(y[c * rows:(c + 1) * rows], g, b) for c in range(LN_CHUNKS)], axis=0)


def _run_interleaved(chains):
    chains = list(chains)
    while chains:
        for chain in list(chains):
            if next(chain, StopIteration) is StopIteration:
                chains.remove(chain)


def _to_interleaved(stage_ref, x):
    for c in range(LANE_BLOCKS):
        for i in range(SUBLANES):
            stage_ref[c, i * RUN_PITCH:i * RUN_PITCH + RUN, :] = (
                x[i * RUN:(i + 1) * RUN, c * LANES:(c + 1) * LANES])
    return jnp.concatenate(
        [jnp.concatenate([stage_ref[c, pl.ds(j, SUBLANES, stride=RUN_PITCH), :]
                          for c in range(LANE_BLOCKS)], axis=-1) for j in range(RUN)], axis=0)


def _from_interleaved(stage_ref, y):
    for j in range(RUN):
        for c in range(LANE_BLOCKS):
            stage_ref[c, pl.ds(j, SUBLANES, stride=RUN_PITCH), :] = (
                y[j * SUBLANES:(j + 1) * SUBLANES, c * LANES:(c + 1) * LANES])
    return jnp.concatenate(
        [jnp.concatenate([stage_ref[c, i * RUN_PITCH:i * RUN_PITCH + RUN, :]
                          for c in range(LANE_BLOCKS)], axis=-1) for i in range(SUBLANES)], axis=0)


def _interleaved_conv(tail_ref, cur, w_ref, width):
    rows, d = cur.shape
    taps = width - 1
    tail = cur[rows - taps * SUBLANES:]
    sub = lax.broadcasted_iota(jnp.int32, (taps, SUBLANES, d), 1)
    before = jnp.where(sub == 0,
                       pltpu.roll(tail_ref[...].reshape(taps, SUBLANES, d), 1, axis=1),
                       pltpu.roll(tail.reshape(taps, SUBLANES, d), 1, axis=1))
    before = before.reshape(taps * SUBLANES, d)
    tail_ref[...] = tail
    y = cur * w_ref[0:1, :]
    for k in range(1, width):
        shifted = jnp.concatenate(
            [before[(taps - k) * SUBLANES:], cur[:rows - k * SUBLANES]], axis=0)
        y = y + shifted * w_ref[k:k + 1, :]
    return y


def _mixer_kernel(x_ref, w_in_ref, cw_ref, cb_ref, w_ri_ref, b_r_ref, b_i_ref, lam_ref,
                  w_lru_out_ref, scw_ref, w_sc_out_ref, b_merge_ref, w_mix_ref, g_ref, b_ref,
                  o_ref, stage, lin_tail, sc_tail, hl_buf, ac_buf, h_carry):
    rows = SEQ_TILE
    crows = rows // LRU_CHUNKS
    cgroups = crows // SUBLANES
    t = pl.program_id(1)

    @pl.when(t == 0)
    def _():
        lin_tail[...] = jnp.zeros_like(lin_tail)
        sc_tail[...] = jnp.zeros_like(sc_tail)
        h_carry[...] = jnp.zeros_like(h_carry)

    nlam = -lam_ref[...]
    softplus = jnp.maximum(nlam, 0.0) + jnp.log1p(jnp.exp(-jnp.abs(nlam)))
    log_a_scale = -LRU_C * softplus

    def chain(s):
        x = _to_interleaved(stage.at[s], x_ref[s * rows:(s + 1) * rows, :])
        xb = x.astype(BF16)

        def seg(i):
            return _dot(xb, w_in_ref[:, i * D_MODEL:(i + 1) * D_MODEL])

        lin = seg(1)
        yield
        u = _interleaved_conv(lin_tail, lin, cw_ref, LRU_CONV) + cb_ref[...]
        ub = u.astype(BF16)
        yield
        z = [_dot(ub[:, h * LRU_BLOCK:(h + 1) * LRU_BLOCK], w_ri_ref[h])
             for h in range(N_LRU_HEADS)]
        zr = jnp.concatenate([zh[:, :LRU_BLOCK] for zh in z], axis=-1) + b_r_ref[...]
        zi = jnp.concatenate([zh[:, LRU_BLOCK:] for zh in z], axis=-1) + b_i_ref[...]
        sc_c = seg(3)
        yield

        def lru_chunk(c, state, produced):
            rs = slice(c * crows, (c + 1) * crows)
            log_a = _after(log_a_scale, produced) * _sigmoid(zr[rs])
            a = jnp.exp(log_a)
            th = jnp.tanh(log_a)
            gain = -2.0 * th / (1.0 - th)
            mult = jnp.where(gain > 0.0, gain * lax.rsqrt(gain), 0.0)
            if c == 0 and s == 0:
                row = lax.broadcasted_iota(jnp.int32, (crows, D_MODEL), 0)
                mult = jnp.where((row == 0) & (t == 0), 1.0, mult)
            bterm = mult * (_sigmoid(zi[rs]) * u[rs])
            a3 = a.reshape(cgroups, SUBLANES, D_MODEL)
            b3 = bterm.reshape(cgroups, SUBLANES, D_MODEL)
            for g in range(cgroups):
                if state is None:
                    hl, ac = b3[g], a3[g]
                else:
                    hl, ac = a3[g] * state[0] + b3[g], a3[g] * state[1]
                hl_buf[s, c * cgroups + g] = hl
                ac_buf[s, c * cgroups + g] = ac
                state = (hl, ac)
            return state

        state = lru_chunk(0, None, sc_c)
        yield
        sc_h = seg(4)
        sc_b = seg(2)
        yield
        state = lru_chunk(1, state, sc_b)
        conv_ch = _interleaved_conv(sc_tail, sc_c * sc_h, scw_ref, SC_CONV)
        sc_in = (sc_b * conv_ch).astype(BF16)
        yield
        y_sc = _dot(sc_in, w_sc_out_ref[...])
        g_sc = seg(6)
        yield
        state = lru_chunk(2, state, g_sc)
        yield
        lru_gate = seg(0)
        yield
        q, p = lru_chunk(3, state, lru_gate)
        sub = lax.broadcasted_iota(jnp.int32, (SUBLANES, D_MODEL), 0)
        for k in (1, 2, 4):
            keep = sub >= k
            q = jnp.where(keep, q + p * pltpu.roll(q, k, axis=0), q)
            p = jnp.where(keep, p * pltpu.roll(p, k, axis=0), p)
        h0 = h_carry[...]
        after = p * h0 + q
        entering = jnp.where(sub == 0, h0, pltpu.roll(after, 1, axis=0))
        h_carry[...] = after[SUBLANES - 1:SUBLANES, :]
        h = (hl_buf[s] + ac_buf[s] * entering).reshape(rows, D_MODEL)
        lru_in = (jax.nn.gelu(lru_gate, approximate=True) * h).astype(BF16)
        yield
        g_lru = seg(5)
        y_lru = _dot(lru_in, w_lru_out_ref[...])
        yield
        merged = (_sigmoid(g_lru + b_merge_ref[0:1, :]) * y_lru
                  + _sigmoid(g_sc + b_merge_ref[1:2, :]) * y_sc).astype(BF16)
        yield
        y = DN_ALPHA * x + _dot(merged, w_mix_ref[...])
        yield
        o_ref[s * rows:(s + 1) * rows, :] = _layer_norm(y, g_ref[...], b_ref[...])

    _run_interleaved(chain(s) for s in range(STEP_TILES))


def _xattn_ffn_kernel(x_ref, mem_ref, w_q_ref, w_k_ref, w_v_ref, w_xo_ref, g2_ref, b2_ref,
                      w_gate_ref, w_up_ref, cw_ref, cb_ref, w_down_ref, g3_ref, b3_ref,
                      o_ref, k_scr, v_scr, stage, gate_tail):
    t = pl.program_id(1)
    n_col = D_FF // D_MODEL
    cols = [slice(j * D_MODEL, (j + 1) * D_MODEL) for j in range(n_col)]
    head_cols = [slice(h * XHEAD_DIM, (h + 1) * XHEAD_DIM) for h in range(N_XHEADS)]

    @pl.when(t == 0)
    def _():
        gate_tail[...] = jnp.zeros_like(gate_tail)
        mb = mem_ref[...].astype(BF16)
        k_scr[...] = _dot(mb, w_k_ref[...]).astype(BF16)
        v_scr[...] = _dot(mb, w_v_ref[...]).astype(BF16)

    def chain(s):
        x = x_ref[s * SEQ_TILE:(s + 1) * SEQ_TILE, :]
        q = _dot(x.astype(BF16), w_q_ref[...])
        yield
        qb = (q * (XHEAD_DIM ** -0.5)).astype(BF16)
        yield
        scores = [lax.dot_general(qb[:, sl], k_scr[:, sl], (((1,), (1,)), ((), ())),
                                  preferred_element_type=F32) for sl in head_cols]
        yield
        probs, denoms = [], []
        for sc in scores:
            p = jnp.exp(sc - jnp.max(sc, axis=-1, keepdims=True))
            denoms.append(jnp.sum(p, axis=-1, keepdims=True))
            probs.append(p.astype(BF16))
        yield
        pv = [_dot(p, v_scr[:, sl]) for p, sl in zip(probs, head_cols)]
        yield
        att = jnp.concatenate([o / d for o, d in zip(pv, denoms)], axis=-1).astype(BF16)
        yield
        y = DN_ALPHA * x + _dot(att, w_xo_ref[...])
        yield
        x = _layer_norm(y, g2_ref[...], b2_ref[...])
        xb = x.astype(BF16)
        yield

        def gate_act(j, pre):
            sl = cols[j]
            gate = _interleaved_conv(gate_tail.at[:, sl], pre, cw_ref.at[:, sl], FFN_CONV)
            return jax.nn.gelu(gate + cb_ref[:, sl], approximate=True)

        acc = DN_ALPHA * x
        pre = _dot(xb, w_gate_ref[:, cols[0]])
        up = _dot(xb, w_up_ref[:, cols[0]])
        yield
        for j in range(n_col):
            this_pre, this_up = pre, up
            if j + 1 < n_col:
                pre = _dot(xb, w_gate_ref[:, cols[j + 1]])
                up = _dot(xb, w_up_ref[:, cols[j + 1]])
            yield
            act = gate_act(j, this_pre)
            yield
            hidden = (act * this_up).astype(BF16)
            yield
            acc = acc + _dot(hidden, w_down_ref[cols[j], :])
        yield
        o_ref[s * SEQ_TILE:(s + 1) * SEQ_TILE, :] = _from_interleaved(
            stage.at[s], _layer_norm(acc, g3_ref[...], b3_ref[...]))

    _run_interleaved(chain(s) for s in range(STEP_TILES))


def _resident(shape):
    zeros = (0,) * len(shape)
    return pl.BlockSpec(shape, lambda b, t: zeros, pipeline_mode=pl.Buffered(1))


def _seq_tiled(rows, cols):
    return pl.BlockSpec((None, rows, cols), lambda b, t: (b, t, 0))


def _call(body, name, x, inputs, scratch_shapes, extra_specs=(), to_bf16=()):
    bsz, seq, d = x.shape
    step_rows = STEP_TILES * SEQ_TILE
    assert seq % step_rows == 0 and d == D_MODEL
    steps_per_batch = seq // step_rows
    n_steps = bsz * steps_per_batch
    n_in = 1 + len(inputs)
    n_cast = len(to_bf16)

    def slab_spec(w):
        assert w.shape[0] % n_steps == 0
        return pl.BlockSpec((w.shape[0] // n_steps, w.shape[1]),
                            lambda b, t: (b * steps_per_batch + t, 0))

    def full_body(*refs):
        cast_in = refs[n_in:n_in + n_cast]
        cast_out = refs[n_in + n_cast + 1:n_in + 2 * n_cast + 1]
        for src, dst in zip(cast_in, cast_out):
            dst[...] = src[...].astype(BF16)
        body(*refs[:n_in], refs[n_in + n_cast], *refs[n_in + 2 * n_cast + 1:])

    resident = inputs[len(extra_specs):]
    return pl.pallas_call(
        full_body,
        name=name,
        grid=(bsz, steps_per_batch),
        in_specs=[_seq_tiled(step_rows, d), *extra_specs, *[_resident(p.shape) for p in resident],
                  *[slab_spec(w) for w in to_bf16]],
        out_specs=[_seq_tiled(step_rows, d), *[slab_spec(w) for w in to_bf16]],
        out_shape=[jax.ShapeDtypeStruct(x.shape, x.dtype),
                   *[jax.ShapeDtypeStruct(w.shape, BF16) for w in to_bf16]],
        scratch_shapes=scratch_shapes,
        compiler_params=pltpu.CompilerParams(
            dimension_semantics=("arbitrary", "arbitrary"),
            vmem_limit_bytes=VMEM_LIMIT_BYTES),
    )(x, *inputs, *to_bf16)


def _row(v):
    return v.reshape(1, -1)


def kernel(x, mem, w_in, lru_conv_w, lru_conv_b, w_rgate, b_rgate, w_igate, b_igate, lru_lambda,
           w_lru_out, sc_conv_w, w_sc_out, b_merge, w_mix_out, ln1_g, ln1_b, w_q, w_k, w_v, w_xo,
           ln2_g, ln2_b, w_ffn_gate, w_ffn_up, ffn_conv_w, ffn_conv_b, w_ffn_down, ln3_g, ln3_b):
    n_mem = mem.shape[1]
    groups = SEQ_TILE // SUBLANES
    for l in range(DEPTH):
        w_ri = jnp.concatenate([w_rgate[l], w_igate[l]], axis=-1).astype(BF16)
        mixer_params = (
            w_in[l].astype(BF16), lru_conv_w[l], _row(lru_conv_b[l]), w_ri, _row(b_rgate[l]),
            _row(b_igate[l]), _row(lru_lambda[l]), w_lru_out[l].astype(BF16), sc_conv_w[l],
            w_sc_out[l].astype(BF16), b_merge[l], w_mix_out[l].astype(BF16), _row(ln1_g[l]),
            _row(ln1_b[l]))
        stage = pltpu.VMEM((STEP_TILES, LANE_BLOCKS, SUBLANES * RUN_PITCH, LANES), F32)
        next_weights = (w_q[l], w_k[l], w_v[l], w_xo[l], w_ffn_gate[l], w_ffn_up[l], w_ffn_down[l])
        x, wq, wk, wv, wxo, wgate, wup, wdown = _call(_mixer_kernel, "mixer", x, mixer_params, [
            stage,
            pltpu.VMEM(((LRU_CONV - 1) * SUBLANES, D_MODEL), F32),
            pltpu.VMEM(((SC_CONV - 1) * SUBLANES, D_MODEL), F32),
            pltpu.VMEM((STEP_TILES, groups, SUBLANES, D_MODEL), F32),
            pltpu.VMEM((STEP_TILES, groups, SUBLANES, D_MODEL), F32),
            pltpu.VMEM((1, D_MODEL), F32),
        ], to_bf16=next_weights)

        xattn_ffn_inputs = (
            mem, wq, wk, wv, wxo, _row(ln2_g[l]), _row(ln2_b[l]), wgate, wup, ffn_conv_w[l],
            _row(ffn_conv_b[l]), wdown, _row(ln3_g[l]), _row(ln3_b[l]))
        mem_spec = pl.BlockSpec((None, n_mem, D_MODEL), lambda b, t: (b, 0, 0))
        x, = _call(_xattn_ffn_kernel, "xattn_convffn", x, xattn_ffn_inputs, [
            pltpu.VMEM((n_mem, D_MODEL), BF16),
            pltpu.VMEM((n_mem, D_MODEL), BF16),
            stage,
            pltpu.VMEM(((FFN_CONV - 1) * SUBLANES, D_FF), F32),
        ], extra_specs=(mem_spec,))
    return x
```
